```python
import math
import jax, jax.numpy as jnp
from jax import lax
import numpy as np

D_MODEL = 1024
BATCH = 1
SEQ = 16384
DEPTH = 1

ROPE_THETA = 10000.0
NORM_EPS = 1e-6
Q_BLOCK = 128

DIFF_HEADS = 4
DIFF_HEAD_DIM = 64
DIFF_V_DIM = 2 * DIFF_HEAD_DIM
DIFF_WIDTH = DIFF_HEADS * DIFF_V_DIM
DIFF_QK_COLS = DIFF_HEADS * 2 * DIFF_HEAD_DIM

MLA_HEADS = 4
MLA_NOPE = 128
MLA_ROPE = 64
MLA_V = 128
MLA_Q_RANK = 384
MLA_KV_RANK = 256
MLA_WIDTH = MLA_HEADS * MLA_V

SPLIT_SIZES = (DIFF_QK_COLS, DIFF_QK_COLS, DIFF_WIDTH, DIFF_WIDTH,
               MLA_Q_RANK, MLA_KV_RANK, MLA_ROPE, MLA_WIDTH,
               D_MODEL, D_MODEL)
IN_COLS = sum(SPLIT_SIZES)

kernel_name = "hybrid_diffattn_mla_gated_block"


def rms_norm(x, gain):
    xf = x.astype(jnp.float32)
    y = xf * lax.rsqrt(jnp.mean(xf * xf, axis=-1, keepdims=True) + NORM_EPS)
    return (y * gain.astype(jnp.float32)).astype(x.dtype)


def rope_tables(positions, dim):
    inv_freq = ROPE_THETA ** (-jnp.arange(0, dim, 2, dtype=jnp.float32) / dim)
    ang = positions.astype(jnp.float32)[..., None] * inv_freq
    return jnp.cos(ang), jnp.sin(ang)


def apply_rope(x, cos, sin):
    b, s, half = cos.shape
    shape = (b,) + (1,) * (x.ndim - 3) + (s, half)
    c = cos.reshape(shape).astype(x.dtype)
    sn = sin.reshape(shape).astype(x.dtype)
    x1, x2 = x[..., :half], x[..., half:]
    return jnp.concatenate([x1 * c - x2 * sn, x2 * c + x1 * sn], axis=-1)


def causal_multimap_attention(q, k, v, map_weights, scale):
    b, h, m, s, dk = q.shape
    dv = v.shape[-1]
    nb = s // Q_BLOCK
    qb = jnp.moveaxis(q.reshape(b, h, m, nb, Q_BLOCK, dk), 3, 0)
    kpos = jnp.arange(s)
    w = map_weights.astype(jnp.float32)

    def one_block(args):
        q_blk, i = args
        qpos = i * Q_BLOCK + jnp.arange(Q_BLOCK)
        sc = jnp.einsum('bhmqd,bhmkd->bhmqk', q_blk, k).astype(jnp.float32) * scale
        sc = jnp.where(kpos[None, :] <= qpos[:, None], sc, -jnp.inf)
        p = jax.nn.softmax(sc, axis=-1)
        p = jnp.einsum('m,bhmqk->bhqk', w, p)
        return jnp.einsum('bhqk,bhkd->bhqd', p.astype(v.dtype), v)

    out = lax.map(one_block, (qb, jnp.arange(nb)))
    return jnp.moveaxis(out, 0, 2).reshape(b, h, s, dv)


def setup_inputs(seed: int = 0) -> dict:
    key = jax.random.key(seed)
    ks = jax.random.split(key, 20)
    L = DEPTH
    nrm = lambda k, shape, fan: jax.random.normal(k, shape, jnp.float32) * fan ** -0.5
    gain = lambda k, n: 1.0 + 0.05 * jax.random.normal(k, (L, n), jnp.float32)
    x = jax.random.normal(ks[0], (BATCH, SEQ, D_MODEL), jnp.float32)
    positions = jnp.broadcast_to(jnp.arange(SEQ, dtype=jnp.int32), (BATCH, SEQ))
    return {
        "x": x,
        "positions": positions,
        "norm_in": gain(ks[1], D_MODEL),
        "w_in": nrm(ks[2], (L, D_MODEL, IN_COLS), D_MODEL),
        "diff_lambda_q1": 0.1 * jax.random.normal(ks[3], (L, DIFF_HEAD_DIM), jnp.float32),
        "diff_lambda_k1": 0.1 * jax.random.normal(ks[4], (L, DIFF_HEAD_DIM), jnp.float32),
        "diff_lambda_q2": 0.1 * jax.random.normal(ks[5], (L, DIFF_HEAD_DIM), jnp.float32),
        "diff_lambda_k2": 0.1 * jax.random.normal(ks[6], (L, DIFF_HEAD_DIM), jnp.float32),
        "diff_subln": gain(ks[7], DIFF_V_DIM),
        "mla_q_norm": gain(ks[8], MLA_Q_RANK),
        "w_uq": nrm(ks[9], (L, MLA_Q_RANK, MLA_HEADS * (MLA_NOPE + MLA_ROPE)), MLA_Q_RANK),
        "mla_kv_norm": gain(ks[10], MLA_KV_RANK),
        "w_ukv": nrm(ks[11], (L, MLA_KV_RANK, MLA_HEADS * (MLA_NOPE + MLA_V)), MLA_KV_RANK),
        "w_proj_diff": nrm(ks[12], (L, DIFF_WIDTH, D_MODEL), DIFF_WIDTH),
        "w_proj_mla": nrm(ks[13], (L, MLA_WIDTH, D_MODEL), MLA_WIDTH),
        "w_out": nrm(ks[14], (L, D_MODEL, D_MODEL), D_MODEL),
        "norm_final": 1.0 + 0.05 * jax.random.normal(ks[15], (D_MODEL,), jnp.float32),
    }


def reference(x, positions, norm_in, w_in, diff_lambda_q1, diff_lambda_k1,
              diff_lambda_q2, diff_lambda_k2, diff_subln, mla_q_norm, w_uq,
              mla_kv_norm, w_ukv, w_proj_diff, w_proj_mla, w_out, norm_final):
    b, s, _ = x.shape
    cos_d, sin_d = rope_tables(positions, DIFF_HEAD_DIM)
    cos_m, sin_m = rope_tables(positions, MLA_ROPE)
    split_idx = list(np.cumsum(SPLIT_SIZES)[:-1])

    for layer in range(DEPTH):
        lambda_init = 0.8 - 0.6 * math.exp(-0.3 * layer)
        h = rms_norm(x, norm_in[layer])
        proj = h @ w_in[layer]
        (dq, dk, dv, dgate, cq, ckv, kr, mgate, g_diff, g_mla) = jnp.split(proj, split_idx, axis=-1)

        dq = dq.reshape(b, s, DIFF_HEADS, 2, DIFF_HEAD_DIM).transpose(0, 2, 3, 1, 4)
        dk = dk.reshape(b, s, DIFF_HEADS, 2, DIFF_HEAD_DIM).transpose(0, 2, 3, 1, 4)
        dv = dv.reshape(b, s, DIFF_HEADS, DIFF_V_DIM).transpose(0, 2, 1, 3)
        dq = apply_rope(dq, cos_d, sin_d)
        dk = apply_rope(dk, cos_d, sin_d)
        lam = (jnp.exp(jnp.sum(diff_lambda_q1[layer] * diff_lambda_k1[layer]))
               - jnp.exp(jnp.sum(diff_lambda_q2[layer] * diff_lambda_k2[layer]))
               + lambda_init)
        weights = jnp.stack([jnp.ones_like(lam), -lam])
        o_diff = causal_multimap_attention(dq, dk, dv, weights, DIFF_HEAD_DIM ** -0.5)
        o_diff = rms_norm(o_diff, diff_subln[layer]) * (1.0 - lambda_init)
        o_diff = o_diff.transpose(0, 2, 1, 3).reshape(b, s, DIFF_WIDTH)
        o_diff = o_diff * jax.nn.silu(dgate)

        cq = rms_norm(cq, mla_q_norm[layer])
        q = (cq @ w_uq[layer]).reshape(b, s, MLA_HEADS, MLA_NOPE + MLA_ROPE).transpose(0, 2, 1, 3)
        q_nope, q_rope = q[..., :MLA_NOPE], apply_rope(q[..., MLA_NOPE:], cos_m, sin_m)
        ckv = rms_norm(ckv, mla_kv_norm[layer])
        kv = (ckv @ w_ukv[layer]).reshape(b, s, MLA_HEADS, MLA_NOPE + MLA_V).transpose(0, 2, 1, 3)
        k_nope, mv = kv[..., :MLA_NOPE], kv[..., MLA_NOPE:]
        k_rope = apply_rope(kr[:, None], cos_m, sin_m)
        mq = jnp.concatenate([q_nope, q_rope], axis=-1)[:, :, None]
        mk = jnp.concatenate([k_nope, jnp.broadcast_to(k_rope, k_nope.shape[:-1] + (MLA_ROPE,))], axis=-1)[:, :, None]
        o_mla = causal_multimap_attention(mq, mk, mv, jnp.ones((1,), jnp.float32),
                                          (MLA_NOPE + MLA_ROPE) ** -0.5)
        o_mla = o_mla.transpose(0, 2, 1, 3).reshape(b, s, MLA_WIDTH)
        o_mla = o_mla * jax.nn.silu(mgate)

        merged = (jax.nn.sigmoid(g_diff) * (o_diff @ w_proj_diff[layer])
                  + jax.nn.sigmoid(g_mla) * (o_mla @ w_proj_mla[layer]))
        x = x + merged @ w_out[layer]

    return rms_norm(x, norm_final)
```

```python
import functools
import math

import jax
import jax.numpy as jnp
from jax import lax
from jax.experimental import pallas as pl
from jax.experimental.pallas import tpu as pltpu

D_MODEL = 1024
ROPE_THETA = 10000.0
NORM_EPS = 1e-6

DIFF_HEADS = 4
DIFF_HEAD_DIM = 64
DIFF_V_DIM = 128
MLA_HEADS = 4
MLA_NOPE = 128
MLA_ROPE = 64
MLA_V = 128
MLA_Q_RANK = 384
MLA_KV_RANK = 256

LANES = 128
LOG2E = math.log2(math.e)
MASK_VALUE = -1e30

ROW_BLOCK = 512
Q_ROWS = 512
KV_ROWS = 512
VMEM_LIMIT_BYTES = 56 * 1024 * 1024

_A_DQ, _A_DK, _A_DV, _A_CQ, _A_CKV, _A_KR, _A_END = 0, 512, 1024, 1536, 1920, 2176, 2304


def _rms(x, gain):
    return x * lax.rsqrt(jnp.mean(x * x, axis=-1, keepdims=True) + NORM_EPS) * gain


def _rope_tile(t, cos, sin_signed, first_half):
    partner = jnp.where(first_half, pltpu.roll(t, 96, axis=1), pltpu.roll(t, 32, axis=1))
    return t * cos + partner * sin_signed


def _input_kernel(x_ref, pos_ref, invf_ref, gin_ref, wa_ref, gq_ref, wuq_ref, gkv_ref,
                  wuk_ref, wuv_ref,
                  qd_ref, kd_ref, vtd_ref, qm_ref, km_ref, vtm_ref,
                  *, diff_scale, mla_scale):
    x = x_ref[...]
    h = _rms(x, gin_ref[...]).astype(jnp.bfloat16)
    proj = jnp.dot(h, wa_ref[...], preferred_element_type=jnp.float32)

    lane = lax.broadcasted_iota(jnp.int32, (1, LANES), 1)
    first_half = (lane & 63) < 32
    ang = pos_ref[...].astype(jnp.float32) * invf_ref[...]
    cos = jnp.cos(ang)
    sin_signed = jnp.sin(ang) * jnp.where(first_half, -1.0, 1.0)
    rope = functools.partial(_rope_tile, cos=cos, sin_signed=sin_signed, first_half=first_half)

    for hd in range(DIFF_HEADS):
        c = hd * LANES
        q = rope(proj[:, _A_DQ + c:_A_DQ + c + LANES]) * diff_scale
        qd_ref[:, c:c + LANES] = q.astype(jnp.bfloat16)
        k = rope(proj[:, _A_DK + c:_A_DK + c + LANES])
        kd_ref[:, c:c + LANES] = k.astype(jnp.bfloat16)
        v = proj[:, _A_DV + c:_A_DV + c + LANES]
        vtd_ref[hd] = v.T.astype(jnp.bfloat16)

    cq = _rms(proj[:, _A_CQ:_A_CKV], gq_ref[...]).astype(jnp.bfloat16)
    qm = jnp.dot(cq, wuq_ref[...], preferred_element_type=jnp.float32)
    ckv = _rms(proj[:, _A_CKV:_A_KR], gkv_ref[...]).astype(jnp.bfloat16)
    kn = jnp.dot(ckv, wuk_ref[...], preferred_element_type=jnp.float32)
    mv = jnp.dot(ckv, wuv_ref[...], preferred_element_type=jnp.float32)
    kr = rope(proj[:, _A_KR:_A_END]).astype(jnp.bfloat16)

    for hd in range(MLA_HEADS):
        c = hd * 2 * LANES
        qm_ref[:, c:c + LANES] = (qm[:, c:c + LANES] * mla_scale).astype(jnp.bfloat16)
        qr = rope(qm[:, c + LANES:c + 2 * LANES]) * mla_scale
        qm_ref[:, c + LANES:c + 2 * LANES] = qr.astype(jnp.bfloat16)
        km_ref[:, c:c + LANES] = kn[:, hd * LANES:(hd + 1) * LANES].astype(jnp.bfloat16)
        km_ref[:, c + LANES:c + 2 * LANES] = kr
        vtm_ref[hd] = mv[:, hd * LANES:(hd + 1) * LANES].T.astype(jnp.bfloat16)


def _input_stage(x2, pos2, invf, gin, wa, gq, wuq, gkv, wuk, wuv):
    s = x2.shape[0]
    tm = ROW_BLOCK
    full = lambda shape: pl.BlockSpec(shape, lambda i: (0,) * len(shape))
    rows = lambda n: pl.BlockSpec((tm, n), lambda i: (i, 0))
    vt_spec = pl.BlockSpec((DIFF_HEADS, LANES, tm), lambda i: (0, 0, i))
    kern = functools.partial(
        _input_kernel,
        diff_scale=DIFF_HEAD_DIM ** -0.5 * LOG2E,
        mla_scale=(MLA_NOPE + MLA_ROPE) ** -0.5 * LOG2E)
    bf = jnp.bfloat16
    return pl.pallas_call(
        kern,
        grid=(s // tm,),
        in_specs=[rows(D_MODEL), rows(1), full(invf.shape), full(gin.shape), full(wa.shape),
                  full(gq.shape), full(wuq.shape), full(gkv.shape), full(wuk.shape),
                  full(wuv.shape)],
        out_specs=[rows(512), rows(512), vt_spec, rows(1024), rows(1024), vt_spec],
        out_shape=[jax.ShapeDtypeStruct((s, 512), bf), jax.ShapeDtypeStruct((s, 512), bf),
                   jax.ShapeDtypeStruct((DIFF_HEADS, LANES, s), bf),
                   jax.ShapeDtypeStruct((s, 1024), bf), jax.ShapeDtypeStruct((s, 1024), bf),
                   jax.ShapeDtypeStruct((MLA_HEADS, LANES, s), bf)],
        compiler_params=pltpu.CompilerParams(
            dimension_semantics=("parallel",), vmem_limit_bytes=VMEM_LIMIT_BYTES),
        name="input_stage",
    )(x2, pos2, invf, gin, wa, gq, wuq, gkv, wuk, wuv)


def _attn_step(qcols, k_ref, vt_ref, m_ref, l_ref, acc_ref, start, rows, mask):
    kb = k_ref[pl.ds(start, rows), :]
    st = lax.dot_general(kb, qcols, (((1,), (1,)), ((), ())),
                         preferred_element_type=jnp.float32)
    if mask is not None:
        st = jnp.where(mask, st, MASK_VALUE)
    m_old = m_ref[...]
    m_new = jnp.maximum(m_old, jnp.max(st, axis=0, keepdims=True))
    alpha = jnp.exp2(m_old - m_new)
    p = jnp.exp2(st - m_new)
    l_ref[...] = alpha * l_ref[...] + jnp.sum(p, axis=0, keepdims=True)
    m_ref[...] = m_new
    vtb = vt_ref[:, pl.ds(start, rows)]
    pv = jnp.dot(vtb, p.astype(jnp.bfloat16), preferred_element_type=jnp.float32)
    acc_ref[...] = alpha * acc_ref[...] + pv


def _attn_kernel(*refs, n_maps, tq, tk):
    if n_maps == 2:
        (q_ref, k_ref, vt_ref, lq1_ref, lk1_ref, lq2_ref, lk2_ref,
         o_ref, m_ref, l_ref, acc_ref) = refs
    else:
        q_ref, k_ref, vt_ref, o_ref, m_ref, l_ref, acc_ref = refs
    qi = pl.program_id(1)
    q = q_ref[...]
    if n_maps == 2:
        lane = lax.broadcasted_iota(jnp.int32, q.shape, 1)
        zero = jnp.zeros_like(q)
        qcols = jnp.concatenate(
            [jnp.where(lane < DIFF_HEAD_DIM, q, zero), jnp.where(lane >= DIFF_HEAD_DIM, q, zero)],
            axis=0)
    else:
        qcols = q
    ncol = n_maps * tq

    m_ref[...] = jnp.full(m_ref.shape, MASK_VALUE, jnp.float32)
    l_ref[...] = jnp.zeros(l_ref.shape, jnp.float32)
    acc_ref[...] = jnp.zeros(acc_ref.shape, jnp.float32)

    def body(j, carry):
        start = pl.multiple_of(j * tk, tk)
        _attn_step(qcols, k_ref, vt_ref, m_ref, l_ref, acc_ref, start, tk, None)
        return carry

    lax.fori_loop(0, qi * (tq // tk), body, 0)

    krow = lax.broadcasted_iota(jnp.int32, (tq, ncol), 0)
    qcol = lax.broadcasted_iota(jnp.int32, (tq, ncol), 1)
    if n_maps == 2:
        qcol = jnp.where(qcol >= tq, qcol - tq, qcol)
    start = pl.multiple_of(qi * tq, tq)
    _attn_step(qcols, k_ref, vt_ref, m_ref, l_ref, acc_ref, start, tq, krow <= qcol)

    out = acc_ref[...] * (1.0 / l_ref[...])
    if n_maps == 2:
        lam_init = 0.8 - 0.6 * math.exp(-0.3 * 0)
        lam = (jnp.exp(jnp.sum(lq1_ref[...] * lk1_ref[...], axis=-1, keepdims=True))
               - jnp.exp(jnp.sum(lq2_ref[...] * lk2_ref[...], axis=-1, keepdims=True))
               + lam_init)
        out = out[:, :tq] - lam * out[:, tq:]
    o_ref[...] = out.T


def _attention(q, k, vt, lambdas, *, n_maps, dk):
    s = q.shape[0]
    heads = vt.shape[0]
    dv = vt.shape[1]
    tq, tk = Q_ROWS, KV_ROWS
    ncol = n_maps * tq
    in_specs = [pl.BlockSpec((tq, dk), lambda h, i: (i, h)),
                pl.BlockSpec((s, dk), lambda h, i: (0, h)),
                pl.BlockSpec((None, dv, s), lambda h, i: (h, 0, 0))]
    in_specs += [pl.BlockSpec(lam.shape, lambda h, i: (0, 0)) for lam in lambdas]
    return pl.pallas_call(
        functools.partial(_attn_kernel, n_maps=n_maps, tq=tq, tk=tk),
        grid=(heads, s // tq),
        in_specs=in_specs,
        out_specs=pl.BlockSpec((tq, dv), lambda h, i: (i, h)),
        out_shape=jax.ShapeDtypeStruct((s, heads * dv), jnp.float32),
        scratch_shapes=[pltpu.VMEM((1, ncol), jnp.float32),
                        pltpu.VMEM((1, ncol), jnp.float32),
                        pltpu.VMEM((dv, ncol), jnp.float32)],
        compiler_params=pltpu.CompilerParams(
            dimension_semantics=("parallel", "parallel"), vmem_limit_bytes=VMEM_LIMIT_BYTES),
        name="diff_attention" if n_maps == 2 else "mla_attention",
    )(q, k, vt, *lambdas)


def _output_kernel(x_ref, od_ref, om_ref, gin_ref, wg_ref, subln_ref, wpd_ref, wpm_ref,
                   wout_ref, gfin_ref, out_ref, *, lam_init):
    x = x_ref[...]
    h = _rms(x, gin_ref[...]).astype(jnp.bfloat16)
    gates = jnp.dot(h, wg_ref[...], preferred_element_type=jnp.float32)
    dgate, mgate = gates[:, 0:512], gates[:, 512:1024]
    g_diff, g_mla = gates[:, 1024:2048], gates[:, 2048:3072]

    od = od_ref[...]
    subln = subln_ref[...]
    od = jnp.concatenate(
        [_rms(od[:, hd * LANES:(hd + 1) * LANES], subln) for hd in range(DIFF_HEADS)], axis=1)
    od = od * (1.0 - lam_init) * jax.nn.silu(dgate)
    om = om_ref[...] * jax.nn.silu(mgate)

    pd = jnp.dot(od.astype(jnp.bfloat16), wpd_ref[...], preferred_element_type=jnp.float32)
    pm = jnp.dot(om.astype(jnp.bfloat16), wpm_ref[...], preferred_element_type=jnp.float32)
    merged = jax.nn.sigmoid(g_diff) * pd + jax.nn.sigmoid(g_mla) * pm
    y = x + jnp.dot(merged.astype(jnp.bfloat16), wout_ref[...],
                    preferred_element_type=jnp.float32)
    out_ref[...] = _rms(y, gfin_ref[...])


def _output_stage(x2, od, om, gin, wg, subln, wpd, wpm, wout, gfin):
    s = x2.shape[0]
    tm = ROW_BLOCK
    full = lambda shape: pl.BlockSpec(shape, lambda i: (0,) * len(shape))
    rows = lambda n: pl.BlockSpec((tm, n), lambda i: (i, 0))
    lam_init = 0.8 - 0.6 * math.exp(-0.3 * 0)
    return pl.pallas_call(
        functools.partial(_output_kernel, lam_init=lam_init),
        grid=(s // tm,),
        in_specs=[rows(D_MODEL), rows(512), rows(512), full(gin.shape), full(wg.shape),
                  full(subln.shape), full(wpd.shape), full(wpm.shape), full(wout.shape),
                  full(gfin.shape)],
        out_specs=rows(D_MODEL),
        out_shape=jax.ShapeDtypeStruct((s, D_MODEL), jnp.float32),
        compiler_params=pltpu.CompilerParams(
            dimension_semantics=("parallel",), vmem_limit_bytes=VMEM_LIMIT_BYTES),
        name="output_stage",
    )(x2, od, om, gin, wg, subln, wpd, wpm, wout, gfin)


def kernel(x, positions, norm_in, w_in, diff_lambda_q1, diff_lambda_k1, diff_lambda_q2,
           diff_lambda_k2, diff_subln, mla_q_norm, w_uq, mla_kv_norm, w_ukv, w_proj_diff,
           w_proj_mla, w_out, norm_final):
    b, s, d = x.shape
    assert b == 1 and d == D_MODEL and w_in.shape[0] == 1
    bf = jnp.bfloat16
    w = w_in[0]
    wa = jnp.concatenate(
        [w[:, 0:1536], w[:, 2048:2752], jnp.zeros((d, 64), w.dtype)], axis=1).astype(bf)
    wg = jnp.concatenate([w[:, 1536:2048], w[:, 2752:5312]], axis=1).astype(bf)
    wuq = w_uq[0].reshape(MLA_Q_RANK, MLA_HEADS, MLA_NOPE + MLA_ROPE)
    wuq = jnp.concatenate(
        [wuq, jnp.zeros((MLA_Q_RANK, MLA_HEADS, 64), wuq.dtype)], axis=2)
    wuq = wuq.reshape(MLA_Q_RANK, MLA_HEADS * 256).astype(bf)
    wukv = w_ukv[0].reshape(MLA_KV_RANK, MLA_HEADS, MLA_NOPE + MLA_V)
    wuk = wukv[:, :, :MLA_NOPE].reshape(MLA_KV_RANK, MLA_HEADS * MLA_NOPE).astype(bf)
    wuv = wukv[:, :, MLA_NOPE:].reshape(MLA_KV_RANK, MLA_HEADS * MLA_V).astype(bf)

    inv_freq = ROPE_THETA ** (-jnp.arange(0, 64, 2, dtype=jnp.float32) / 64)
    invf = jnp.tile(inv_freq, 4).reshape(1, LANES)

    x2 = x.reshape(s, d)
    pos2 = positions.reshape(s, 1)
    qd, kd, vtd, qm, km, vtm = _input_stage(
        x2, pos2, invf, norm_in, wa, mla_q_norm, wuq, mla_kv_norm, wuk, wuv)

    lambdas = (diff_lambda_q1, diff_lambda_k1, diff_lambda_q2, diff_lambda_k2)
    od = _attention(qd, kd, vtd, lambdas, n_maps=2, dk=LANES)
    om = _attention(qm, km, vtm, (), n_maps=1, dk=2 * LANES)

    out = _output_stage(x2, od, om, norm_in, wg, diff_subln, w_proj_diff[0].astype(bf),
                        w_proj_mla[0].astype(bf), w_out[0].astype(bf),
                        norm_final.reshape(1, d))
    return out.reshape(b, s, d)
```

```python
import functools
import math

import jax
import jax.numpy as jnp
from jax import lax
from jax.experimental import pallas as pl
from jax.experimental.pallas import tpu as pltpu

D_MODEL = 1024
ROPE_THETA = 10000.0
NORM_EPS = 1e-6

DIFF_HEADS = 4
DIFF_HEAD_DIM = 64
DIFF_V_DIM = 128
MLA_HEADS = 4
MLA_NOPE = 128
MLA_ROPE = 64
MLA_V = 128
MLA_Q_RANK = 384
MLA_KV_RANK = 256

LANES = 128
LOG2E = math.log2(math.e)
MASK_VALUE = -1e30

ROW_BLOCK = 512
Q_ROWS = 512
VMEM_LIMIT_BYTES = 56 * 1024 * 1024

_A_DQ, _A_DK, _A_DV, _A_CQ, _A_CKV, _A_KR, _A_END = 0, 512, 1024, 1536, 1920, 2176, 2304


def _rms(x, gain):
    return x * lax.rsqrt(jnp.mean(x * x, axis=-1, keepdims=True) + NORM_EPS) * gain


def _rope_tile(t, cos, sin_signed, first_half):
    partner = jnp.where(first_half, pltpu.roll(t, 96, axis=1), pltpu.roll(t, 32, axis=1))
    return t * cos + partner * sin_signed


def _input_kernel(x_ref, pos_ref, invf_ref, gin_ref, wa_ref, gq_ref, wuq_ref, gkv_ref,
                  wuk_ref, wuv_ref,
                  qd_ref, kd_ref, vtd_ref, qm_ref, km_ref, vtm_ref,
                  *, diff_scale, mla_scale):
    x = x_ref[...]
    h = _rms(x, gin_ref[...]).astype(jnp.bfloat16)
    proj = jnp.dot(h, wa_ref[...], preferred_element_type=jnp.float32)

    lane = lax.broadcasted_iota(jnp.int32, (1, LANES), 1)
    first_half = (lane & 63) < 32
    ang = pos_ref[...].astype(jnp.float32) * invf_ref[...]
    cos = jnp.cos(ang)
    sin_signed = jnp.sin(ang) * jnp.where(first_half, -1.0, 1.0)
    rope = functools.partial(_rope_tile, cos=cos, sin_signed=sin_signed, first_half=first_half)

    for hd in range(DIFF_HEADS):
        c = hd * LANES
        q = rope(proj[:, _A_DQ + c:_A_DQ + c + LANES]) * diff_scale
        qd_ref[:, c:c + LANES] = q.astype(jnp.bfloat16)
        k = rope(proj[:, _A_DK + c:_A_DK + c + LANES])
        kd_ref[:, c:c + LANES] = k.astype(jnp.bfloat16)
        v = proj[:, _A_DV + c:_A_DV + c + LANES]
        vtd_ref[hd] = v.T.astype(jnp.bfloat16)

    cq = _rms(proj[:, _A_CQ:_A_CKV], gq_ref[...]).astype(jnp.bfloat16)
    qm = jnp.dot(cq, wuq_ref[...], preferred_element_type=jnp.float32)
    ckv = _rms(proj[:, _A_CKV:_A_KR], gkv_ref[...]).astype(jnp.bfloat16)
    kn = jnp.dot(ckv, wuk_ref[...], preferred_element_type=jnp.float32)
    mv = jnp.dot(ckv, wuv_ref[...], preferred_element_type=jnp.float32)
    kr = rope(proj[:, _A_KR:_A_END]).astype(jnp.bfloat16)

    for hd in range(MLA_HEADS):
        c = hd * 2 * LANES
        qm_ref[:, c:c + LANES] = (qm[:, c:c + LANES] * mla_scale).astype(jnp.bfloat16)
        qr = rope(qm[:, c + LANES:c + 2 * LANES]) * mla_scale
        qm_ref[:, c + LANES:c + 2 * LANES] = qr.astype(jnp.bfloat16)
        km_ref[:, c:c + LANES] = kn[:, hd * LANES:(hd + 1) * LANES].astype(jnp.bfloat16)
        km_ref[:, c + LANES:c + 2 * LANES] = kr
        vtm_ref[hd] = mv[:, hd * LANES:(hd + 1) * LANES].T.astype(jnp.bfloat16)


def _input_stage(x2, pos2, invf, gin, wa, gq, wuq, gkv, wuk, wuv):
    s = x2.shape[0]
    tm = ROW_BLOCK
    full = lambda shape: pl.BlockSpec(shape, lambda i: (0,) * len(shape))
    rows = lambda n: pl.BlockSpec((tm, n), lambda i: (i, 0))
    vt_spec = pl.BlockSpec((DIFF_HEADS, LANES, tm), lambda i: (0, 0, i))
    kern = functools.partial(
        _input_kernel,
        diff_scale=DIFF_HEAD_DIM ** -0.5 * LOG2E,
        mla_scale=(MLA_NOPE + MLA_ROPE) ** -0.5 * LOG2E)
    bf = jnp.bfloat16
    return pl.pallas_call(
        kern,
        grid=(s // tm,),
        in_specs=[rows(D_MODEL), rows(1), full(invf.shape), full(gin.shape), full(wa.shape),
                  full(gq.shape), full(wuq.shape), full(gkv.shape), full(wuk.shape),
                  full(wuv.shape)],
        out_specs=[rows(512), rows(512), vt_spec, rows(1024), rows(1024), vt_spec],
        out_shape=[jax.ShapeDtypeStruct((s, 512), bf), jax.ShapeDtypeStruct((s, 512), bf),
                   jax.ShapeDtypeStruct((DIFF_HEADS, LANES, s), bf),
                   jax.ShapeDtypeStruct((s, 1024), bf), jax.ShapeDtypeStruct((s, 1024), bf),
                   jax.ShapeDtypeStruct((MLA_HEADS, LANES, s), bf)],
        compiler_params=pltpu.CompilerParams(
            dimension_semantics=("parallel",), vmem_limit_bytes=VMEM_LIMIT_BYTES),
        name="input_stage",
    )(x2, pos2, invf, gin, wa, gq, wuq, gkv, wuk, wuv)


class _AttnRefs:
    def __init__(self, k_ref, vt_ref, m_ref, l_ref, acc_ref, s_refs, bmax_refs, p_refs, alpha_refs):
        self.k, self.vt, self.m, self.l, self.acc = k_ref, vt_ref, m_ref, l_ref, acc_ref
        self.s, self.bmax, self.p, self.alpha = s_refs, bmax_refs, p_refs, alpha_refs


def _score_stage(r, qcols, start, slot, mask):
    kb = r.k[pl.ds(start, r.s[slot].shape[0]), :]
    st = lax.dot_general(kb, qcols, (((1,), (1,)), ((), ())),
                         preferred_element_type=jnp.float32)
    if mask is not None:
        st = jnp.where(mask, st, MASK_VALUE)
    r.s[slot][...] = st
    r.bmax[slot][...] = jnp.max(st, axis=0, keepdims=True)


def _softmax_stage(r, slot):
    m_old = r.m[...]
    m_new = jnp.maximum(m_old, r.bmax[slot][...])
    alpha = jnp.exp2(m_old - m_new)
    p = jnp.exp2(r.s[slot][...] - m_new)
    r.l[...] = alpha * r.l[...] + jnp.sum(p, axis=0, keepdims=True)
    r.m[...] = m_new
    r.p[slot][...] = p.astype(jnp.bfloat16)
    r.alpha[slot][...] = alpha


def _value_stage(r, start, slot):
    vtb = r.vt[:, pl.ds(start, r.p[slot].shape[0])]
    pv = jnp.dot(vtb, r.p[slot][...], preferred_element_type=jnp.float32)
    r.acc[...] = r.alpha[slot][...] * r.acc[...] + pv


def _attn_kernel(*refs, n_maps, tq):
    if n_maps == 2:
        q_ref, k_ref, vt_ref, lq1_ref, lk1_ref, lq2_ref, lk2_ref, o_ref = refs[:8]
        scratch = refs[8:]
    else:
        q_ref, k_ref, vt_ref, o_ref = refs[:4]
        scratch = refs[4:]
    m_ref, l_ref, acc_ref, s0, s1, b0, b1, p0, p1, a0, a1 = scratch
    r = _AttnRefs(k_ref, vt_ref, m_ref, l_ref, acc_ref, (s0, s1), (b0, b1), (p0, p1), (a0, a1))

    qi = pl.program_id(1)
    nblk = qi + 1
    q = q_ref[...]
    if n_maps == 2:
        lane = lax.broadcasted_iota(jnp.int32, q.shape, 1)
        zero = jnp.zeros_like(q)
        qcols = jnp.concatenate(
            [jnp.where(lane < DIFF_HEAD_DIM, q, zero), jnp.where(lane >= DIFF_HEAD_DIM, q, zero)],
            axis=0)
    else:
        qcols = q
    ncol = n_maps * tq

    def key_start(t):
        blk = jnp.where(t == 0, qi, t - 1)
        return pl.multiple_of(blk * tq, tq)

    m_ref[...] = jnp.full(m_ref.shape, MASK_VALUE, jnp.float32)
    l_ref[...] = jnp.zeros(l_ref.shape, jnp.float32)
    acc_ref[...] = jnp.zeros(acc_ref.shape, jnp.float32)

    krow = lax.broadcasted_iota(jnp.int32, (tq, ncol), 0)
    qcol = lax.broadcasted_iota(jnp.int32, (tq, ncol), 1)
    if n_maps == 2:
        qcol = jnp.where(qcol >= tq, qcol - tq, qcol)
    _score_stage(r, qcols, key_start(0), 0, krow <= qcol)

    def full_step(t, par):
        _score_stage(r, qcols, key_start(t), par, None)
        _softmax_stage(r, 1 - par)
        _value_stage(r, key_start(t - 2), par)

    @pl.when(nblk == 1)
    def _():
        _softmax_stage(r, 0)
        _value_stage(r, key_start(0), 0)

    @pl.when(nblk >= 2)
    def _():
        _score_stage(r, qcols, key_start(1), 1, None)
        _softmax_stage(r, 0)

        def pair(u, carry):
            t = 2 * u + 2
            full_step(t, 0)
            full_step(t + 1, 1)
            return carry

        nfull = nblk - 2
        lax.fori_loop(0, nfull // 2, pair, 0)

        @pl.when((nblk & 1) == 0)
        def _():
            _softmax_stage(r, 1)
            _value_stage(r, key_start(nblk - 2), 0)
            _value_stage(r, key_start(nblk - 1), 1)

        @pl.when((nblk & 1) == 1)
        def _():
            full_step(nblk - 1, 0)
            _softmax_stage(r, 0)
            _value_stage(r, key_start(nblk - 2), 1)
            _value_stage(r, key_start(nblk - 1), 0)

    out = acc_ref[...] * (1.0 / l_ref[...])
    if n_maps == 2:
        lam_init = 0.8 - 0.6 * math.exp(-0.3 * 0)
        lam = (jnp.exp(jnp.sum(lq1_ref[...] * lk1_ref[...], axis=-1, keepdims=True))
               - jnp.exp(jnp.sum(lq2_ref[...] * lk2_ref[...], axis=-1, keepdims=True))
               + lam_init)
        out = out[:, :tq] - lam * out[:, tq:]
    o_ref[...] = out.T


def _attention(q, k, vt, lambdas, *, n_maps, dk):
    s = q.shape[0]
    heads = vt.shape[0]
    dv = vt.shape[1]
    tq = Q_ROWS
    ncol = n_maps * tq
    in_specs = [pl.BlockSpec((tq, dk), lambda h, i: (i, h)),
                pl.BlockSpec((s, dk), lambda h, i: (0, h)),
                pl.BlockSpec((None, dv, s), lambda h, i: (h, 0, 0))]
    in_specs += [pl.BlockSpec(lam.shape, lambda h, i: (0, 0)) for lam in lambdas]
    f32, bf = jnp.float32, jnp.bfloat16
    stat = pltpu.VMEM((1, ncol), f32)
    scratch = [stat, stat, pltpu.VMEM((dv, ncol), f32),
               pltpu.VMEM((tq, ncol), f32), pltpu.VMEM((tq, ncol), f32), stat, stat,
               pltpu.VMEM((tq, ncol), bf), pltpu.VMEM((tq, ncol), bf), stat, stat]
    return pl.pallas_call(
        functools.partial(_attn_kernel, n_maps=n_maps, tq=tq),
        grid=(heads, s // tq),
        in_specs=in_specs,
        out_specs=pl.BlockSpec((tq, dv), lambda h, i: (i, h)),
        out_shape=jax.ShapeDtypeStruct((s, heads * dv), jnp.float32),
        scratch_shapes=scratch,
        compiler_params=pltpu.CompilerParams(
            dimension_semantics=("parallel", "parallel"), vmem_limit_bytes=VMEM_LIMIT_BYTES),
        name="diff_attention" if n_maps == 2 else "mla_attention",
    )(q, k, vt, *lambdas)


def _output_kernel(x_ref, od_ref, om_ref, gin_ref, wg_ref, subln_ref, wpd_ref, wpm_ref,
                   wout_ref, gfin_ref, out_ref, *, lam_init):
    x = x_ref[...]
    h = _rms(x, gin_ref[...]).astype(jnp.bfloat16)
    gates = jnp.dot(h, wg_ref[...], preferred_element_type=jnp.float32)
    dgate, mgate = gates[:, 0:512], gates[:, 512:1024]
    g_diff, g_mla = gates[:, 1024:2048], gates[:, 2048:3072]

    od = od_ref[...]
    subln = subln_ref[...]
    od = jnp.concatenate(
        [_rms(od[:, hd * LANES:(hd + 1) * LANES], subln) for hd in range(DIFF_HEADS)], axis=1)
    od = od * (1.0 - lam_init) * jax.nn.silu(dgate)
    om = om_ref[...] * jax.nn.silu(mgate)

    pd = jnp.dot(od.astype(jnp.bfloat16), wpd_ref[...], preferred_element_type=jnp.float32)
    pm = jnp.dot(om.astype(jnp.bfloat16), wpm_ref[...], preferred_element_type=jnp.float32)
    merged = jax.nn.sigmoid(g_diff) * pd + jax.nn.sigmoid(g_mla) * pm
    y = x + jnp.dot(merged.astype(jnp.bfloat16), wout_ref[...],
                    preferred_element_type=jnp.float32)
    out_ref[...] = _rms(y, gfin_ref[...])


def _output_stage(x2, od, om, gin, wg, subln, wpd, wpm, wout, gfin):
    s = x2.shape[0]
    tm = ROW_BLOCK
    full = lambda shape: pl.BlockSpec(shape, lambda i: (0,) * len(shape))
    rows = lambda n: pl.BlockSpec((tm, n), lambda i: (i, 0))
    lam_init = 0.8 - 0.6 * math.exp(-0.3 * 0)
    return pl.pallas_call(
        functools.partial(_output_kernel, lam_init=lam_init),
        grid=(s // tm,),
        in_specs=[rows(D_MODEL), rows(512), rows(512), full(gin.shape), full(wg.shape),
                  full(subln.shape), full(wpd.shape), full(wpm.shape), full(wout.shape),
                  full(gfin.shape)],
        out_specs=rows(D_MODEL),
        out_shape=jax.ShapeDtypeStruct((s, D_MODEL), jnp.float32),
        compiler_params=pltpu.CompilerParams(
            dimension_semantics=("parallel",), vmem_limit_bytes=VMEM_LIMIT_BYTES),
        name="output_stage",
    )(x2, od, om, gin, wg, subln, wpd, wpm, wout, gfin)


def kernel(x, positions, norm_in, w_in, diff_lambda_q1, diff_lambda_k1, diff_lambda_q2,
           diff_lambda_k2, diff_subln, mla_q_norm, w_uq, mla_kv_norm, w_ukv, w_proj_diff,
           w_proj_mla, w_out, norm_final):
    b, s, d = x.shape
    assert b == 1 and d == D_MODEL and w_in.shape[0] == 1
    bf = jnp.bfloat16
    w = w_in[0]
    wa = jnp.concatenate(
        [w[:, 0:1536], w[:, 2048:2752], jnp.zeros((d, 64), w.dtype)], axis=1).astype(bf)
    wg = jnp.concatenate([w[:, 1536:2048], w[:, 2752:5312]], axis=1).astype(bf)
    wuq = w_uq[0].reshape(MLA_Q_RANK, MLA_HEADS, MLA_NOPE + MLA_ROPE)
    wuq = jnp.concatenate(
        [wuq, jnp.zeros((MLA_Q_RANK, MLA_HEADS, 64), wuq.dtype)], axis=2)
    wuq = wuq.reshape(MLA_Q_RANK, MLA_HEADS * 256).astype(bf)
    wukv = w_ukv[0].reshape(MLA_KV_RANK, MLA_HEADS, MLA_NOPE + MLA_V)
    wuk = wukv[:, :, :MLA_NOPE].reshape(MLA_KV_RANK, MLA_HEADS * MLA_NOPE).astype(bf)
    wuv = wukv[:, :, MLA_NOPE:].reshape(MLA_KV_RANK, MLA_HEADS * MLA_V).astype(bf)

    inv_freq = ROPE_THETA ** (-jnp.arange(0, 64, 2, dtype=jnp.float32) / 64)
    invf = jnp.tile(inv_freq, 4).reshape(1, LANES)

    x2 = x.reshape(s, d)
    pos2 = positions.reshape(s, 1)
    qd, kd, vtd, qm, km, vtm = _input_stage(
        x2, pos2, invf, norm_in, wa, mla_q_norm, wuq, mla_kv_norm, wuk, wuv)

    lambdas = (diff_lambda_q1, diff_lambda_k1, diff_lambda_q2, diff_lambda_k2)
    od = _attention(qd, kd, vtd, lambdas, n_maps=2, dk=LANES)
    om = _attention(qm, km, vtm, (), n_maps=1, dk=2 * LANES)

    out = _output_stage(x2, od, om, norm_in, wg, diff_subln, w_proj_diff[0].astype(bf),
                        w_proj_mla[0].astype(bf), w_out[0].astype(bf),
                        norm_final.reshape(1, d))
    return out.reshape(b, s, d)
```

```python
import functools
import math

import jax
import jax.numpy as jnp
from jax import lax
from jax.experimental import pallas as pl
from jax.experimental.pallas import tpu as pltpu

D_MODEL = 1024
ROPE_THETA = 10000.0
NORM_EPS = 1e-6

DIFF_HEADS = 4
DIFF_HEAD_DIM = 64
DIFF_V_DIM = 128
MLA_HEADS = 4
MLA_NOPE = 128
MLA_ROPE = 64
MLA_V = 128
MLA_Q_RANK = 384
MLA_KV_RANK = 256

LANES = 128
LOG2E = math.log2(math.e)
MASK_VALUE = -1e30

ROW_BLOCK = 512
KEY_ROWS = 512
CHUNK_COLS = 256
SUM_ROWS = 16
VT_ROWS = MLA_V + SUM_ROWS
VMEM_LIMIT_BYTES = 56 * 1024 * 1024

_A_DQ, _A_DK, _A_DV, _A_CQ, _A_CKV, _A_KR, _A_END = 0, 512, 1024, 1536, 1920, 2176, 2304


def _rms(x, gain):
    return x * lax.rsqrt(jnp.mean(x * x, axis=-1, keepdims=True) + NORM_EPS) * gain


def _rope_tile(t, cos, sin_signed, first_half):
    partner = jnp.where(first_half, pltpu.roll(t, 96, axis=1), pltpu.roll(t, 32, axis=1))
    return t * cos + partner * sin_signed


def _input_kernel(x_ref, pos_ref, invf_ref, gin_ref, wa_ref, gq_ref, wuq_ref, gkv_ref,
                  wuk_ref, wuv_ref,
                  qd_ref, kd_ref, vtd_ref, qm_ref, km_ref, vtm_ref,
                  *, diff_scale, mla_scale):
    x = x_ref[...]
    h = _rms(x, gin_ref[...]).astype(jnp.bfloat16)
    proj = jnp.dot(h, wa_ref[...], preferred_element_type=jnp.float32)

    lane = lax.broadcasted_iota(jnp.int32, (1, LANES), 1)
    first_half = (lane & 63) < 32
    ang = pos_ref[...].astype(jnp.float32) * invf_ref[...]
    cos = jnp.cos(ang)
    sin_signed = jnp.sin(ang) * jnp.where(first_half, -1.0, 1.0)
    rope = functools.partial(_rope_tile, cos=cos, sin_signed=sin_signed, first_half=first_half)
    ones = jnp.ones((SUM_ROWS, x.shape[0]), jnp.bfloat16)

    for hd in range(DIFF_HEADS):
        c = hd * LANES
        q = rope(proj[:, _A_DQ + c:_A_DQ + c + LANES]) * diff_scale
        qd_ref[:, c:c + LANES] = q.astype(jnp.bfloat16)
        k = rope(proj[:, _A_DK + c:_A_DK + c + LANES])
        kd_ref[:, c:c + LANES] = k.astype(jnp.bfloat16)
        v = proj[:, _A_DV + c:_A_DV + c + LANES]
        vtd_ref[hd, :DIFF_V_DIM, :] = v.T.astype(jnp.bfloat16)
        vtd_ref[hd, DIFF_V_DIM:, :] = ones

    cq = _rms(proj[:, _A_CQ:_A_CKV], gq_ref[...]).astype(jnp.bfloat16)
    qm = jnp.dot(cq, wuq_ref[...], preferred_element_type=jnp.float32)
    ckv = _rms(proj[:, _A_CKV:_A_KR], gkv_ref[...]).astype(jnp.bfloat16)
    kn = jnp.dot(ckv, wuk_ref[...], preferred_element_type=jnp.float32)
    mv = jnp.dot(ckv, wuv_ref[...], preferred_element_type=jnp.float32)
    kr = rope(proj[:, _A_KR:_A_END]).astype(jnp.bfloat16)

    for hd in range(MLA_HEADS):
        c = hd * 2 * LANES
        qm_ref[:, c:c + LANES] = (qm[:, c:c + LANES] * mla_scale).astype(jnp.bfloat16)
        qr = rope(qm[:, c + LANES:c + 2 * LANES]) * mla_scale
        qm_ref[:, c + LANES:c + 2 * LANES] = qr.astype(jnp.bfloat16)
        km_ref[:, c:c + LANES] = kn[:, hd * LANES:(hd + 1) * LANES].astype(jnp.bfloat16)
        km_ref[:, c + LANES:c + 2 * LANES] = kr
        vtm_ref[hd, :MLA_V, :] = mv[:, hd * LANES:(hd + 1) * LANES].T.astype(jnp.bfloat16)
        vtm_ref[hd, MLA_V:, :] = ones


def _input_stage(x2, pos2, invf, gin, wa, gq, wuq, gkv, wuk, wuv):
    s = x2.shape[0]
    tm = ROW_BLOCK
    full = lambda shape: pl.BlockSpec(shape, lambda i: (0,) * len(shape))
    rows = lambda n: pl.BlockSpec((tm, n), lambda i: (i, 0))
    vt_spec = pl.BlockSpec((DIFF_HEADS, VT_ROWS, tm), lambda i: (0, 0, i))
    kern = functools.partial(
        _input_kernel,
        diff_scale=DIFF_HEAD_DIM ** -0.5 * LOG2E,
        mla_scale=(MLA_NOPE + MLA_ROPE) ** -0.5 * LOG2E)
    bf = jnp.bfloat16
    return pl.pallas_call(
        kern,
        grid=(s // tm,),
        in_specs=[rows(D_MODEL), rows(1), full(invf.shape), full(gin.shape), full(wa.shape),
                  full(gq.shape), full(wuq.shape), full(gkv.shape), full(wuk.shape),
                  full(wuv.shape)],
        out_specs=[rows(512), rows(512), vt_spec, rows(1024), rows(1024), vt_spec],
        out_shape=[jax.ShapeDtypeStruct((s, 512), bf), jax.ShapeDtypeStruct((s, 512), bf),
                   jax.ShapeDtypeStruct((DIFF_HEADS, VT_ROWS, s), bf),
                   jax.ShapeDtypeStruct((s, 1024), bf), jax.ShapeDtypeStruct((s, 1024), bf),
                   jax.ShapeDtypeStruct((MLA_HEADS, VT_ROWS, s), bf)],
        compiler_params=pltpu.CompilerParams(
            dimension_semantics=("parallel",), vmem_limit_bytes=VMEM_LIMIT_BYTES),
        name="input_stage",
    )(x2, pos2, invf, gin, wa, gq, wuq, gkv, wuk, wuv)


class _AttnRefs:
    def __init__(self, k_ref, vt_ref, qc_ref, m_ref, acc_ref, s_refs, bmax_refs, p_refs,
                 alpha_refs):
        self.k, self.vt, self.qc, self.m, self.acc = k_ref, vt_ref, qc_ref, m_ref, acc_ref
        self.s, self.bmax, self.p, self.alpha = s_refs, bmax_refs, p_refs, alpha_refs


def _score_stage(r, start, slot, mask, cols=slice(None)):
    kb = r.k[pl.ds(start, r.s[slot].shape[0]), :]
    st = lax.dot_general(kb, r.qc[cols, :], (((1,), (1,)), ((), ())),
                         preferred_element_type=jnp.float32)
    if mask is not None:
        st = jnp.where(mask[:, cols], st, MASK_VALUE)
    r.s[slot][:, cols] = st
    r.bmax[slot][:, cols] = jnp.max(st, axis=0, keepdims=True)


def _softmax_stage(r, slot, cols=slice(None)):
    m_old = r.m[:, cols]
    m_new = jnp.maximum(m_old, r.bmax[slot][:, cols])
    r.alpha[slot][:, cols] = jnp.exp2(m_old - m_new)
    r.m[:, cols] = m_new
    r.p[slot][:, cols] = jnp.exp2(r.s[slot][:, cols] - m_new).astype(jnp.bfloat16)


def _value_stage(r, start, slot, cols=slice(None)):
    vtb = r.vt[:, pl.ds(start, r.p[slot].shape[0])]
    pv = jnp.dot(vtb, r.p[slot][:, cols], preferred_element_type=jnp.float32)
    r.acc[:, cols] = r.alpha[slot][:, cols] * r.acc[:, cols] + pv


def _attn_kernel(*refs, n_maps, tq, tk):
    if n_maps == 2:
        q_ref, k_ref, vt_ref, lq1_ref, lk1_ref, lq2_ref, lk2_ref, o_ref = refs[:8]
        scratch = refs[8:]
    else:
        q_ref, k_ref, vt_ref, o_ref = refs[:4]
        scratch = refs[4:]
    qc_ref, m_ref, acc_ref, s0, s1, b0, b1, p0, p1, a0, a1 = scratch
    r = _AttnRefs(k_ref, vt_ref, qc_ref, m_ref, acc_ref, (s0, s1), (b0, b1), (p0, p1), (a0, a1))

    rq = tq // tk
    qi = pl.program_id(1)
    nblk = (qi + 1) * rq
    q = q_ref[...]
    if n_maps == 2:
        lane = lax.broadcasted_iota(jnp.int32, q.shape, 1)
        zero = jnp.zeros_like(q)
        qc_ref[:tq, :] = jnp.where(lane < DIFF_HEAD_DIM, q, zero)
        qc_ref[tq:, :] = jnp.where(lane >= DIFF_HEAD_DIM, q, zero)
    else:
        qc_ref[...] = q
    ncol = n_maps * tq

    def key_start(t):
        blk = jnp.where(t < rq, qi * rq + t, t - rq)
        return pl.multiple_of(blk * tk, tk)

    m_ref[...] = jnp.full(m_ref.shape, MASK_VALUE, jnp.float32)
    acc_ref[...] = jnp.zeros(acc_ref.shape, jnp.float32)

    krow = lax.broadcasted_iota(jnp.int32, (tk, ncol), 0)
    qcol = lax.broadcasted_iota(jnp.int32, (tk, ncol), 1)
    if n_maps == 2:
        qcol = jnp.where(qcol >= tq, qcol - tq, qcol)
    masks = [krow + d * tk <= qcol for d in range(rq)] + [None]
    _score_stage(r, key_start(0), 0, masks[0])

    def full_step(t, par):
        for c in range(ncol // CHUNK_COLS):
            cols = slice(c * CHUNK_COLS, (c + 1) * CHUNK_COLS)
            _softmax_stage(r, 1 - par, cols)
            _score_stage(r, key_start(t), par, None, cols)
            _value_stage(r, key_start(t - 2), par, cols)

    def single_block():
        _softmax_stage(r, 0)
        _value_stage(r, key_start(0), 0)

    def even_tail():
        _softmax_stage(r, 1)
        _value_stage(r, key_start(nblk - 2), 0)
        _value_stage(r, key_start(nblk - 1), 1)

    def odd_tail():
        full_step(nblk - 1, 0)
        _softmax_stage(r, 0)
        _value_stage(r, key_start(nblk - 2), 1)
        _value_stage(r, key_start(nblk - 1), 0)

    def multi_block():
        _score_stage(r, key_start(1), 1, masks[min(1, rq)])
        _softmax_stage(r, 0)

        def pair(u, carry):
            t = 2 * u + 2
            full_step(t, 0)
            full_step(t + 1, 1)
            return carry

        nfull = nblk - 2
        lax.fori_loop(0, nfull // 2, pair, 0)
        if rq % 2 == 0:
            even_tail()
        else:
            pl.when((nblk & 1) == 0)(even_tail)
            pl.when((nblk & 1) == 1)(odd_tail)

    if rq == 1:
        pl.when(nblk == 1)(single_block)
        pl.when(nblk >= 2)(multi_block)
    else:
        assert rq == 2
        multi_block()

    dv = o_ref.shape[1]
    out = acc_ref[:dv, :] * (1.0 / acc_ref[dv:dv + 1, :])
    if n_maps == 2:
        lam_init = 0.8 - 0.6 * math.exp(-0.3 * 0)
        lam = (jnp.exp(jnp.sum(lq1_ref[...] * lk1_ref[...], axis=-1, keepdims=True))
               - jnp.exp(jnp.sum(lq2_ref[...] * lk2_ref[...], axis=-1, keepdims=True))
               + lam_init)
        out = out[:, :tq] - lam * out[:, tq:]
    o_ref[...] = out.T


def _attention(q, k, vt, lambdas, *, n_maps, dk, tq):
    s = q.shape[0]
    heads, vt_rows, _ = vt.shape
    dv = vt_rows - SUM_ROWS
    tk = KEY_ROWS
    ncol = n_maps * tq
    in_specs = [pl.BlockSpec((tq, dk), lambda h, i: (i, h)),
                pl.BlockSpec((s, dk), lambda h, i: (0, h)),
                pl.BlockSpec((None, vt_rows, s), lambda h, i: (h, 0, 0))]
    in_specs += [pl.BlockSpec(lam.shape, lambda h, i: (0, 0)) for lam in lambdas]
    f32, bf = jnp.float32, jnp.bfloat16
    stat = pltpu.VMEM((1, ncol), f32)
    scratch = [pltpu.VMEM((ncol, dk), bf), stat, pltpu.VMEM((vt_rows, ncol), f32),
               pltpu.VMEM((tk, ncol), f32), pltpu.VMEM((tk, ncol), f32), stat, stat,
               pltpu.VMEM((tk, ncol), bf), pltpu.VMEM((tk, ncol), bf), stat, stat]
    return pl.pallas_call(
        functools.partial(_attn_kernel, n_maps=n_maps, tq=tq, tk=tk),
        grid=(heads, s // tq),
        in_specs=in_specs,
        out_specs=pl.BlockSpec((tq, dv), lambda h, i: (i, h)),
        out_shape=jax.ShapeDtypeStruct((s, heads * dv), jnp.float32),
        scratch_shapes=scratch,
        compiler_params=pltpu.CompilerParams(
            dimension_semantics=("parallel", "parallel"), vmem_limit_bytes=VMEM_LIMIT_BYTES),
        name="diff_attention" if n_maps == 2 else "mla_attention",
    )(q, k, vt, *lambdas)


def _output_kernel(x_ref, od_ref, om_ref, gin_ref, wg_ref, subln_ref, wpd_ref, wpm_ref,
                   wout_ref, gfin_ref, out_ref, *, lam_init):
    x = x_ref[...]
    h = _rms(x, gin_ref[...]).astype(jnp.bfloat16)
    gates = jnp.dot(h, wg_ref[...], preferred_element_type=jnp.float32)
    dgate, mgate = gates[:, 0:512], gates[:, 512:1024]
    g_diff, g_mla = gates[:, 1024:2048], gates[:, 2048:3072]

    od = od_ref[...]
    subln = subln_ref[...]
    od = jnp.concatenate(
        [_rms(od[:, hd * LANES:(hd + 1) * LANES], subln) for hd in range(DIFF_HEADS)], axis=1)
    od = od * (1.0 - lam_init) * jax.nn.silu(dgate)
    om = om_ref[...] * jax.nn.silu(mgate)

    pd = jnp.dot(od.astype(jnp.bfloat16), wpd_ref[...], preferred_element_type=jnp.float32)
    pm = jnp.dot(om.astype(jnp.bfloat16), wpm_ref[...], preferred_element_type=jnp.float32)
    merged = jax.nn.sigmoid(g_diff) * pd + jax.nn.sigmoid(g_mla) * pm
    y = x + jnp.dot(merged.astype(jnp.bfloat16), wout_ref[...],
                    preferred_element_type=jnp.float32)
    out_ref[...] = _rms(y, gfin_ref[...])


def _output_stage(x2, od, om, gin, wg, subln, wpd, wpm, wout, gfin):
    s = x2.shape[0]
    tm = ROW_BLOCK
    full = lambda shape: pl.BlockSpec(shape, lambda i: (0,) * len(shape))
    rows = lambda n: pl.BlockSpec((tm, n), lambda i: (i, 0))
    lam_init = 0.8 - 0.6 * math.exp(-0.3 * 0)
    return pl.pallas_call(
        functools.partial(_output_kernel, lam_init=lam_init),
        grid=(s // tm,),
        in_specs=[rows(D_MODEL), rows(512), rows(512), full(gin.shape), full(wg.shape),
                  full(subln.shape), full(wpd.shape), full(wpm.shape), full(wout.shape),
                  full(gfin.shape)],
        out_specs=rows(D_MODEL),
        out_shape=jax.ShapeDtypeStruct((s, D_MODEL), jnp.float32),
        compiler_params=pltpu.CompilerParams(
            dimension_semantics=("parallel",), vmem_limit_bytes=VMEM_LIMIT_BYTES),
        name="output_stage",
    )(x2, od, om, gin, wg, subln, wpd, wpm, wout, gfin)


def kernel(x, positions, norm_in, w_in, diff_lambda_q1, diff_lambda_k1, diff_lambda_q2,
           diff_lambda_k2, diff_subln, mla_q_norm, w_uq, mla_kv_norm, w_ukv, w_proj_diff,
           w_proj_mla, w_out, norm_final):
    b, s, d = x.shape
    assert b == 1 and d == D_MODEL and w_in.shape[0] == 1
    bf = jnp.bfloat16
    w = w_in[0]
    wa = jnp.concatenate(
        [w[:, 0:1536], w[:, 2048:2752], jnp.zeros((d, 64), w.dtype)], axis=1).astype(bf)
    wg = jnp.concatenate([w[:, 1536:2048], w[:, 2752:5312]], axis=1).astype(bf)
    wuq = w_uq[0].reshape(MLA_Q_RANK, MLA_HEADS, MLA_NOPE + MLA_ROPE)
    wuq = jnp.concatenate(
        [wuq, jnp.zeros((MLA_Q_RANK, MLA_HEADS, 64), wuq.dtype)], axis=2)
    wuq = wuq.reshape(MLA_Q_RANK, MLA_HEADS * 256).astype(bf)
    wukv = w_ukv[0].reshape(MLA_KV_RANK, MLA_HEADS, MLA_NOPE + MLA_V)
    wuk = wukv[:, :, :MLA_NOPE].reshape(MLA_KV_RANK, MLA_HEADS * MLA_NOPE).astype(bf)
    wuv = wukv[:, :, MLA_NOPE:].reshape(MLA_KV_RANK, MLA_HEADS * MLA_V).astype(bf)

    inv_freq = ROPE_THETA ** (-jnp.arange(0, 64, 2, dtype=jnp.float32) / 64)
    invf = jnp.tile(inv_freq, 4).reshape(1, LANES)

    x2 = x.reshape(s, d)
    pos2 = positions.reshape(s, 1)
    qd, kd, vtd, qm, km, vtm = _input_stage(
        x2, pos2, invf, norm_in, wa, mla_q_norm, wuq, mla_kv_norm, wuk, wuv)

    lambdas = (diff_lambda_q1, diff_lambda_k1, diff_lambda_q2, diff_lambda_k2)
    od = _attention(qd, kd, vtd, lambdas, n_maps=2, dk=LANES, tq=KEY_ROWS)
    om = _attention(qm, km, vtm, (), n_maps=1, dk=2 * LANES, tq=2 * KEY_ROWS)

    out = _output_stage(x2, od, om, norm_in, wg, diff_subln, w_proj_diff[0].astype(bf),
                        w_proj_mla[0].astype(bf), w_out[0].astype(bf),
                        norm_final.reshape(1, d))
    return out.reshape(b, s, d)
```

```python
import functools
import math

import jax
import jax.numpy as jnp
from jax import lax
from jax.experimental import pallas as pl
from jax.experimental.pallas import tpu as pltpu

D_MODEL = 1024
ROPE_THETA = 10000.0
NORM_EPS = 1e-6

DIFF_HEADS = 4
DIFF_HEAD_DIM = 64
DIFF_V_DIM = 128
MLA_HEADS = 4
MLA_NOPE = 128
MLA_ROPE = 64
MLA_V = 128
MLA_Q_RANK = 384
MLA_KV_RANK = 256

LANES = 128
LOG2E = math.log2(math.e)
MASK_VALUE = -1e30

ROW_BLOCK = 512
KEY_ROWS = 512
CHUNK_COLS = 256
SUM_ROWS = 16
VT_ROWS = MLA_V + SUM_ROWS
VMEM_LIMIT_BYTES = 56 * 1024 * 1024

_A_DQ, _A_DK, _A_DV, _A_CQ, _A_CKV, _A_KR, _A_END = 0, 512, 1024, 1536, 1920, 2176, 2304


def _rms(x, gain):
    return x * lax.rsqrt(jnp.mean(x * x, axis=-1, keepdims=True) + NORM_EPS) * gain


def _rope_tile(t, cos, sin_signed, first_half):
    partner = jnp.where(first_half, pltpu.roll(t, 96, axis=1), pltpu.roll(t, 32, axis=1))
    return t * cos + partner * sin_signed


def _input_kernel(x_ref, pos_ref, invf_ref, gin_ref, wa_ref, gq_ref, wuq_ref, gkv_ref,
                  wuk_ref, wuv_ref,
                  qd_ref, kd_ref, vtd_ref, qm_ref, km_ref, vtm_ref,
                  *, diff_scale, mla_scale):
    x = x_ref[...]
    h = _rms(x, gin_ref[...]).astype(jnp.bfloat16)
    proj = jnp.dot(h, wa_ref[...], preferred_element_type=jnp.float32)

    lane = lax.broadcasted_iota(jnp.int32, (1, LANES), 1)
    first_half = (lane & 63) < 32
    ang = pos_ref[...].astype(jnp.float32) * invf_ref[...]
    cos = jnp.cos(ang)
    sin_signed = jnp.sin(ang) * jnp.where(first_half, -1.0, 1.0)
    rope = functools.partial(_rope_tile, cos=cos, sin_signed=sin_signed, first_half=first_half)
    ones = jnp.ones((SUM_ROWS, x.shape[0]), jnp.bfloat16)

    for hd in range(DIFF_HEADS):
        c = hd * LANES
        q = rope(proj[:, _A_DQ + c:_A_DQ + c + LANES]) * diff_scale
        qd_ref[:, c:c + LANES] = q.astype(jnp.bfloat16)
        k = rope(proj[:, _A_DK + c:_A_DK + c + LANES])
        kd_ref[:, c:c + LANES] = k.astype(jnp.bfloat16)
        v = proj[:, _A_DV + c:_A_DV + c + LANES]
        vtd_ref[hd, :DIFF_V_DIM, :] = v.T.astype(jnp.bfloat16)
        vtd_ref[hd, DIFF_V_DIM:, :] = ones

    cq = _rms(proj[:, _A_CQ:_A_CKV], gq_ref[...]).astype(jnp.bfloat16)
    qm = jnp.dot(cq, wuq_ref[...], preferred_element_type=jnp.float32)
    ckv = _rms(proj[:, _A_CKV:_A_KR], gkv_ref[...]).astype(jnp.bfloat16)
    kn = jnp.dot(ckv, wuk_ref[...], preferred_element_type=jnp.float32)
    mv = jnp.dot(ckv, wuv_ref[...], preferred_element_type=jnp.float32)
    kr = rope(proj[:, _A_KR:_A_END]).astype(jnp.bfloat16)

    for hd in range(MLA_HEADS):
        c = hd * 2 * LANES
        qm_ref[:, c:c + LANES] = (qm[:, c:c + LANES] * mla_scale).astype(jnp.bfloat16)
        qr = rope(qm[:, c + LANES:c + 2 * LANES]) * mla_scale
        qm_ref[:, c + LANES:c + 2 * LANES] = qr.astype(jnp.bfloat16)
        km_ref[:, c:c + LANES] = kn[:, hd * LANES:(hd + 1) * LANES].astype(jnp.bfloat16)
        km_ref[:, c + LANES:c + 2 * LANES] = kr
        vtm_ref[hd, :MLA_V, :] = mv[:, hd * LANES:(hd + 1) * LANES].T.astype(jnp.bfloat16)
        vtm_ref[hd, MLA_V:, :] = ones


def _input_stage(x2, pos2, invf, gin, wa, gq, wuq, gkv, wuk, wuv):
    s = x2.shape[0]
    tm = ROW_BLOCK
    full = lambda shape: pl.BlockSpec(shape, lambda i: (0,) * len(shape))
    rows = lambda n: pl.BlockSpec((tm, n), lambda i: (i, 0))
    vt_spec = pl.BlockSpec((DIFF_HEADS, VT_ROWS, tm), lambda i: (0, 0, i))
    kern = functools.partial(
        _input_kernel,
        diff_scale=DIFF_HEAD_DIM ** -0.5 * LOG2E,
        mla_scale=(MLA_NOPE + MLA_ROPE) ** -0.5 * LOG2E)
    bf = jnp.bfloat16
    return pl.pallas_call(
        kern,
        grid=(s // tm,),
        in_specs=[rows(D_MODEL), rows(1), full(invf.shape), full(gin.shape), full(wa.shape),
                  full(gq.shape), full(wuq.shape), full(gkv.shape), full(wuk.shape),
                  full(wuv.shape)],
        out_specs=[rows(512), rows(512), vt_spec, rows(1024), rows(1024), vt_spec],
        out_shape=[jax.ShapeDtypeStruct((s, 512), bf), jax.ShapeDtypeStruct((s, 512), bf),
                   jax.ShapeDtypeStruct((DIFF_HEADS, VT_ROWS, s), bf),
                   jax.ShapeDtypeStruct((s, 1024), bf), jax.ShapeDtypeStruct((s, 1024), bf),
                   jax.ShapeDtypeStruct((MLA_HEADS, VT_ROWS, s), bf)],
        compiler_params=pltpu.CompilerParams(
            dimension_semantics=("parallel",), vmem_limit_bytes=VMEM_LIMIT_BYTES),
        name="input_stage",
    )(x2, pos2, invf, gin, wa, gq, wuq, gkv, wuk, wuv)


class _AttnRefs:
    def __init__(self, k_ref, vt_ref, qc_ref, m_ref, acc_ref, s_refs, bmax_refs, p_refs,
                 alpha_refs):
        self.k, self.vt, self.qc, self.m, self.acc = k_ref, vt_ref, qc_ref, m_ref, acc_ref
        self.s, self.bmax, self.p, self.alpha = s_refs, bmax_refs, p_refs, alpha_refs


def _score_stage(r, start, slot, mask, cols=slice(None)):
    kb = r.k[pl.ds(start, r.s[slot].shape[0]), :]
    st = lax.dot_general(kb, r.qc[cols, :], (((1,), (1,)), ((), ())),
                         preferred_element_type=jnp.float32)
    if mask is not None:
        st = jnp.where(mask[:, cols], st, MASK_VALUE)
    r.s[slot][:, cols] = st
    r.bmax[slot][:, cols] = jnp.max(st, axis=0, keepdims=True)


def _softmax_stage(r, slot, cols=slice(None)):
    m_old = r.m[:, cols]
    m_new = jnp.maximum(m_old, r.bmax[slot][:, cols])
    r.alpha[slot][:, cols] = jnp.exp2(m_old - m_new)
    r.m[:, cols] = m_new
    r.p[slot][:, cols] = jnp.exp2(r.s[slot][:, cols] - m_new).astype(jnp.bfloat16)


def _value_stage(r, start, slot, cols=slice(None)):
    vtb = r.vt[:, pl.ds(start, r.p[slot].shape[0])]
    pv = jnp.dot(vtb, r.p[slot][:, cols], preferred_element_type=jnp.float32)
    r.acc[:, cols] = r.alpha[slot][:, cols] * r.acc[:, cols] + pv


def _attn_kernel(*refs, n_maps, tq, tk, unroll):
    if n_maps == 2:
        q_ref, k_ref, vt_ref, lq1_ref, lk1_ref, lq2_ref, lk2_ref, o_ref = refs[:8]
        scratch = refs[8:]
    else:
        q_ref, k_ref, vt_ref, o_ref = refs[:4]
        scratch = refs[4:]
    qc_ref, m_ref, acc_ref, s0, s1, b0, b1, p0, p1, a0, a1 = scratch
    r = _AttnRefs(k_ref, vt_ref, qc_ref, m_ref, acc_ref, (s0, s1), (b0, b1), (p0, p1), (a0, a1))

    rq = tq // tk
    qi = pl.program_id(1)
    nblk = (qi + 1) * rq
    q = q_ref[...]
    if n_maps == 2:
        lane = lax.broadcasted_iota(jnp.int32, q.shape, 1)
        zero = jnp.zeros_like(q)
        qc_ref[:tq, :] = jnp.where(lane < DIFF_HEAD_DIM, q, zero)
        qc_ref[tq:, :] = jnp.where(lane >= DIFF_HEAD_DIM, q, zero)
    else:
        qc_ref[...] = q
    ncol = n_maps * tq

    def key_start(t):
        blk = jnp.where(t < rq, qi * rq + t, t - rq)
        return pl.multiple_of(blk * tk, tk)

    m_ref[...] = jnp.full(m_ref.shape, MASK_VALUE, jnp.float32)
    acc_ref[...] = jnp.zeros(acc_ref.shape, jnp.float32)

    krow = lax.broadcasted_iota(jnp.int32, (tk, ncol), 0)
    qcol = lax.broadcasted_iota(jnp.int32, (tk, ncol), 1)
    if n_maps == 2:
        qcol = jnp.where(qcol >= tq, qcol - tq, qcol)
    masks = [krow + d * tk <= qcol for d in range(rq)] + [None]
    _score_stage(r, key_start(0), 0, masks[0])

    def full_step(t, par):
        for c in range(ncol // CHUNK_COLS):
            cols = slice(c * CHUNK_COLS, (c + 1) * CHUNK_COLS)
            _softmax_stage(r, 1 - par, cols)
            _score_stage(r, key_start(t), par, None, cols)
            _value_stage(r, key_start(t - 2), par, cols)

    def single_block():
        _softmax_stage(r, 0)
        _value_stage(r, key_start(0), 0)

    def even_tail():
        _softmax_stage(r, 1)
        _value_stage(r, key_start(nblk - 2), 0)
        _value_stage(r, key_start(nblk - 1), 1)

    def odd_tail():
        full_step(nblk - 1, 0)
        _softmax_stage(r, 0)
        _value_stage(r, key_start(nblk - 2), 1)
        _value_stage(r, key_start(nblk - 1), 0)

    def multi_block():
        _score_stage(r, key_start(1), 1, masks[min(1, rq)])
        _softmax_stage(r, 0)

        def steps_from(t, nsteps):
            for i in range(nsteps):
                full_step(t + i, i % 2)

        def group(u, carry):
            steps_from(unroll * u + 2, unroll)
            return carry

        npairs = (nblk - 2) // 2
        ngroups = npairs // (unroll // 2)
        lax.fori_loop(0, ngroups, group, 0)
        if unroll == 4:
            pl.when(npairs - 2 * ngroups == 1)(lambda: steps_from(unroll * ngroups + 2, 2))
        if rq % 2 == 0:
            even_tail()
        else:
            pl.when((nblk & 1) == 0)(even_tail)
            pl.when((nblk & 1) == 1)(odd_tail)

    if rq == 1:
        pl.when(nblk == 1)(single_block)
        pl.when(nblk >= 2)(multi_block)
    else:
        assert rq == 2
        multi_block()

    dv = o_ref.shape[1]
    out = acc_ref[:dv, :] * (1.0 / acc_ref[dv:dv + 1, :])
    if n_maps == 2:
        lam_init = 0.8 - 0.6 * math.exp(-0.3 * 0)
        lam = (jnp.exp(jnp.sum(lq1_ref[...] * lk1_ref[...], axis=-1, keepdims=True))
               - jnp.exp(jnp.sum(lq2_ref[...] * lk2_ref[...], axis=-1, keepdims=True))
               + lam_init)
        out = out[:, :tq] - lam * out[:, tq:]
    o_ref[...] = out.T


def _attention(q, k, vt, lambdas, *, n_maps, dk, tq, unroll):
    s = q.shape[0]
    heads, vt_rows, _ = vt.shape
    dv = vt_rows - SUM_ROWS
    tk = KEY_ROWS
    ncol = n_maps * tq
    in_specs = [pl.BlockSpec((tq, dk), lambda h, i: (i, h)),
                pl.BlockSpec((s, dk), lambda h, i: (0, h)),
                pl.BlockSpec((None, vt_rows, s), lambda h, i: (h, 0, 0))]
    in_specs += [pl.BlockSpec(lam.shape, lambda h, i: (0, 0)) for lam in lambdas]
    f32, bf = jnp.float32, jnp.bfloat16
    stat = pltpu.VMEM((1, ncol), f32)
    scratch = [pltpu.VMEM((ncol, dk), bf), stat, pltpu.VMEM((vt_rows, ncol), f32),
               pltpu.VMEM((tk, ncol), f32), pltpu.VMEM((tk, ncol), f32), stat, stat,
               pltpu.VMEM((tk, ncol), bf), pltpu.VMEM((tk, ncol), bf), stat, stat]
    return pl.pallas_call(
        functools.partial(_attn_kernel, n_maps=n_maps, tq=tq, tk=tk, unroll=unroll),
        grid=(heads, s // tq),
        in_specs=in_specs,
        out_specs=pl.BlockSpec((tq, dv), lambda h, i: (i, h)),
        out_shape=jax.ShapeDtypeStruct((s, heads * dv), jnp.float32),
        scratch_shapes=scratch,
        compiler_params=pltpu.CompilerParams(
            dimension_semantics=("parallel", "parallel"), vmem_limit_bytes=VMEM_LIMIT_BYTES),
        name="diff_attention" if n_maps == 2 else "mla_attention",
    )(q, k, vt, *lambdas)


def _output_kernel(x_ref, od_ref, om_ref, gin_ref, wg_ref, subln_ref, wpd_ref, wpm_ref,
                   wout_ref, gfin_ref, out_ref, *, lam_init):
    x = x_ref[...]
    h = _rms(x, gin_ref[...]).astype(jnp.bfloat16)
    gates = jnp.dot(h, wg_ref[...], preferred_element_type=jnp.float32)
    dgate, mgate = gates[:, 0:512], gates[:, 512:1024]
    g_diff, g_mla = gates[:, 1024:2048], gates[:, 2048:3072]

    od = od_ref[...]
    subln = subln_ref[...]
    od = jnp.concatenate(
        [_rms(od[:, hd * LANES:(hd + 1) * LANES], subln) for hd in range(DIFF_HEADS)], axis=1)
    od = od * (1.0 - lam_init) * jax.nn.silu(dgate)
    om = om_ref[...] * jax.nn.silu(mgate)

    pd = jnp.dot(od.astype(jnp.bfloat16), wpd_ref[...], preferred_element_type=jnp.float32)
    pm = jnp.dot(om.astype(jnp.bfloat16), wpm_ref[...], preferred_element_type=jnp.float32)
    merged = jax.nn.sigmoid(g_diff) * pd + jax.nn.sigmoid(g_mla) * pm
    y = x + jnp.dot(merged.astype(jnp.bfloat16), wout_ref[...],
                    preferred_element_type=jnp.float32)
    out_ref[...] = _rms(y, gfin_ref[...])


def _output_stage(x2, od, om, gin, wg, subln, wpd, wpm, wout, gfin):
    s = x2.shape[0]
    tm = ROW_BLOCK
    full = lambda shape: pl.BlockSpec(shape, lambda i: (0,) * len(shape))
    rows = lambda n: pl.BlockSpec((tm, n), lambda i: (i, 0))
    lam_init = 0.8 - 0.6 * math.exp(-0.3 * 0)
    return pl.pallas_call(
        functools.partial(_output_kernel, lam_init=lam_init),
        grid=(s // tm,),
        in_specs=[rows(D_MODEL), rows(512), rows(512), full(gin.shape), full(wg.shape),
                  full(subln.shape), full(wpd.shape), full(wpm.shape), full(wout.shape),
                  full(gfin.shape)],
        out_specs=rows(D_MODEL),
        out_shape=jax.ShapeDtypeStruct((s, D_MODEL), jnp.float32),
        compiler_params=pltpu.CompilerParams(
            dimension_semantics=("parallel",), vmem_limit_bytes=VMEM_LIMIT_BYTES),
        name="output_stage",
    )(x2, od, om, gin, wg, subln, wpd, wpm, wout, gfin)


def kernel(x, positions, norm_in, w_in, diff_lambda_q1, diff_lambda_k1, diff_lambda_q2,
           diff_lambda_k2, diff_subln, mla_q_norm, w_uq, mla_kv_norm, w_ukv, w_proj_diff,
           w_proj_mla, w_out, norm_final):
    b, s, d = x.shape
    assert b == 1 and d == D_MODEL and w_in.shape[0] == 1
    bf = jnp.bfloat16
    w = w_in[0]
    wa = jnp.concatenate(
        [w[:, 0:1536], w[:, 2048:2752], jnp.zeros((d, 64), w.dtype)], axis=1).astype(bf)
    wg = jnp.concatenate([w[:, 1536:2048], w[:, 2752:5312]], axis=1).astype(bf)
    wuq = w_uq[0].reshape(MLA_Q_RANK, MLA_HEADS, MLA_NOPE + MLA_ROPE)
    wuq = jnp.concatenate(
        [wuq, jnp.zeros((MLA_Q_RANK, MLA_HEADS, 64), wuq.dtype)], axis=2)
    wuq = wuq.reshape(MLA_Q_RANK, MLA_HEADS * 256).astype(bf)
    wukv = w_ukv[0].reshape(MLA_KV_RANK, MLA_HEADS, MLA_NOPE + MLA_V)
    wuk = wukv[:, :, :MLA_NOPE].reshape(MLA_KV_RANK, MLA_HEADS * MLA_NOPE).astype(bf)
    wuv = wukv[:, :, MLA_NOPE:].reshape(MLA_KV_RANK, MLA_HEADS * MLA_V).astype(bf)

    inv_freq = ROPE_THETA ** (-jnp.arange(0, 64, 2, dtype=jnp.float32) / 64)
    invf = jnp.tile(inv_freq, 4).reshape(1, LANES)

    x2 = x.reshape(s, d)
    pos2 = positions.reshape(s, 1)
    qd, kd, vtd, qm, km, vtm = _input_stage(
        x2, pos2, invf, norm_in, wa, mla_q_norm, wuq, mla_kv_norm, wuk, wuv)

    lambdas = (diff_lambda_q1, diff_lambda_k1, diff_lambda_q2, diff_lambda_k2)
    od = _attention(qd, kd, vtd, lambdas, n_maps=2, dk=LANES, tq=2 * KEY_ROWS, unroll=2)
    om = _attention(qm, km, vtm, (), n_maps=1, dk=2 * LANES, tq=2 * KEY_ROWS, unroll=2)

    out = _output_stage(x2, od, om, norm_in, wg, diff_subln, w_proj_diff[0].astype(bf),
                        w_proj_mla[0].astype(bf), w_out[0].astype(bf),
                        norm_final.reshape(1, d))
    return out.reshape(b, s, d)
```

```python
import functools
import math

import jax
import jax.numpy as jnp
from jax import lax
from jax.experimental import pallas as pl
from jax.experimental.pallas import tpu as pltpu

D_MODEL = 1024
ROPE_THETA = 10000.0
NORM_EPS = 1e-6

DIFF_HEADS = 4
DIFF_HEAD_DIM = 64
DIFF_V_DIM = 128
MLA_HEADS = 4
MLA_NOPE = 128
MLA_ROPE = 64
MLA_V = 128
MLA_Q_RANK = 384
MLA_KV_RANK = 256

LANES = 128
LOG2E = math.log2(math.e)
MASK_VALUE = -1e30

ROW_BLOCK = 512
KEY_ROWS = 512
CHUNK_COLS = 256
SUM_ROWS = 16
VT_ROWS = MLA_V + SUM_ROWS
VMEM_LIMIT_BYTES = 56 * 1024 * 1024

_A_DQ, _A_DK, _A_DV, _A_CQ, _A_CKV, _A_KR, _A_END = 0, 512, 1024, 1536, 1920, 2176, 2304


def _rms(x, gain):
    return x * lax.rsqrt(jnp.mean(x * x, axis=-1, keepdims=True) + NORM_EPS) * gain


def _rope_tile(t, cos, sin_signed, first_half):
    partner = jnp.where(first_half, pltpu.roll(t, 96, axis=1), pltpu.roll(t, 32, axis=1))
    return t * cos + partner * sin_signed


def _input_kernel(x_ref, pos_ref, invf_ref, gin_ref, wa_ref, gq_ref, wuq_ref, gkv_ref,
                  wuk_ref, wuv_ref,
                  qd_ref, kd_ref, vtd_ref, qm_ref, km_ref, vtm_ref,
                  *, diff_scale, mla_scale):
    x = x_ref[...]
    h = _rms(x, gin_ref[...]).astype(jnp.bfloat16)
    proj = jnp.dot(h, wa_ref[...], preferred_element_type=jnp.float32)

    lane = lax.broadcasted_iota(jnp.int32, (1, LANES), 1)
    first_half = (lane & 63) < 32
    ang = pos_ref[...].astype(jnp.float32) * invf_ref[...]
    cos = jnp.cos(ang)
    sin_signed = jnp.sin(ang) * jnp.where(first_half, -1.0, 1.0)
    rope = functools.partial(_rope_tile, cos=cos, sin_signed=sin_signed, first_half=first_half)
    ones = jnp.ones((SUM_ROWS, x.shape[0]), jnp.bfloat16)

    for hd in range(DIFF_HEADS):
        c = hd * LANES
        q = rope(proj[:, _A_DQ + c:_A_DQ + c + LANES]) * diff_scale
        qd_ref[:, c:c + LANES] = q.astype(jnp.bfloat16)
        k = rope(proj[:, _A_DK + c:_A_DK + c + LANES])
        kd_ref[:, c:c + LANES] = k.astype(jnp.bfloat16)
        v = proj[:, _A_DV + c:_A_DV + c + LANES]
        vtd_ref[hd, :DIFF_V_DIM, :] = v.T.astype(jnp.bfloat16)
        vtd_ref[hd, DIFF_V_DIM:, :] = ones

    cq = _rms(proj[:, _A_CQ:_A_CKV], gq_ref[...]).astype(jnp.bfloat16)
    qm = jnp.dot(cq, wuq_ref[...], preferred_element_type=jnp.float32)
    ckv = _rms(proj[:, _A_CKV:_A_KR], gkv_ref[...]).astype(jnp.bfloat16)
    kn = jnp.dot(ckv, wuk_ref[...], preferred_element_type=jnp.float32)
    mv = jnp.dot(ckv, wuv_ref[...], preferred_element_type=jnp.float32)
    kr = rope(proj[:, _A_KR:_A_END]).astype(jnp.bfloat16)

    for hd in range(MLA_HEADS):
        c = hd * 2 * LANES
        qm_ref[:, c:c + LANES] = (qm[:, c:c + LANES] * mla_scale).astype(jnp.bfloat16)
        qr = rope(qm[:, c + LANES:c + 2 * LANES]) * mla_scale
        qm_ref[:, c + LANES:c + 2 * LANES] = qr.astype(jnp.bfloat16)
        km_ref[:, c:c + LANES] = kn[:, hd * LANES:(hd + 1) * LANES].astype(jnp.bfloat16)
        km_ref[:, c + LANES:c + 2 * LANES] = kr
        vtm_ref[hd, :MLA_V, :] = mv[:, hd * LANES:(hd + 1) * LANES].T.astype(jnp.bfloat16)
        vtm_ref[hd, MLA_V:, :] = ones


def _input_stage(x2, pos2, invf, gin, wa, gq, wuq, gkv, wuk, wuv):
    s = x2.shape[0]
    tm = ROW_BLOCK
    full = lambda shape: pl.BlockSpec(shape, lambda i: (0,) * len(shape))
    rows = lambda n: pl.BlockSpec((tm, n), lambda i: (i, 0))
    vt_spec = pl.BlockSpec((DIFF_HEADS, VT_ROWS, tm), lambda i: (0, 0, i))
    kern = functools.partial(
        _input_kernel,
        diff_scale=DIFF_HEAD_DIM ** -0.5 * LOG2E,
        mla_scale=(MLA_NOPE + MLA_ROPE) ** -0.5 * LOG2E)
    bf = jnp.bfloat16
    return pl.pallas_call(
        kern,
        grid=(s // tm,),
        in_specs=[rows(D_MODEL), rows(1), full(invf.shape), full(gin.shape), full(wa.shape),
                  full(gq.shape), full(wuq.shape), full(gkv.shape), full(wuk.shape),
                  full(wuv.shape)],
        out_specs=[rows(512), rows(512), vt_spec, rows(1024), rows(1024), vt_spec],
        out_shape=[jax.ShapeDtypeStruct((s, 512), bf), jax.ShapeDtypeStruct((s, 512), bf),
                   jax.ShapeDtypeStruct((DIFF_HEADS, VT_ROWS, s), bf),
                   jax.ShapeDtypeStruct((s, 1024), bf), jax.ShapeDtypeStruct((s, 1024), bf),
                   jax.ShapeDtypeStruct((MLA_HEADS, VT_ROWS, s), bf)],
        compiler_params=pltpu.CompilerParams(
            dimension_semantics=("parallel",), vmem_limit_bytes=VMEM_LIMIT_BYTES),
        name="input_stage",
    )(x2, pos2, invf, gin, wa, gq, wuq, gkv, wuk, wuv)


class _AttnRefs:
    def __init__(self, k_ref, vt_ref, qc_ref, m_ref, acc_ref, s_refs, bmax_refs, p_refs,
                 alpha_refs):
        self.k, self.vt, self.qc, self.m, self.acc = k_ref, vt_ref, qc_ref, m_ref, acc_ref
        self.s, self.bmax, self.p, self.alpha = s_refs, bmax_refs, p_refs, alpha_refs


def _score_stage(r, start, slot, mask, cols, head):
    dk = r.qc.shape[1]
    kb = r.k[pl.ds(start, r.s[slot].shape[0]), head * dk:(head + 1) * dk]
    st = lax.dot_general(kb, r.qc[cols, :], (((1,), (1,)), ((), ())),
                         preferred_element_type=jnp.float32)
    if mask is not None:
        st = jnp.where(mask[:, cols], st, MASK_VALUE)
    r.s[slot][:, cols] = st
    r.bmax[slot][:, cols] = jnp.max(st, axis=0, keepdims=True)


def _softmax_stage(r, slot, cols):
    m_old = r.m[:, cols]
    m_new = jnp.maximum(m_old, r.bmax[slot][:, cols])
    r.alpha[slot][:, cols] = jnp.exp2(m_old - m_new)
    r.m[:, cols] = m_new
    r.p[slot][:, cols] = jnp.exp2(r.s[slot][:, cols] - m_new).astype(jnp.bfloat16)


def _value_stage(r, start, slot, cols, head):
    vtb = r.vt[head, :, pl.ds(start, r.p[slot].shape[0])]
    pv = jnp.dot(vtb, r.p[slot][:, cols], preferred_element_type=jnp.float32)
    r.acc[:, cols] = r.alpha[slot][:, cols] * r.acc[:, cols] + pv


def _attn_kernel(*refs, heads, n_maps, tq, tk):
    n_lam = 4 if n_maps == 2 else 0
    q_ref, k_ref, vt_ref = refs[:3]
    lam_refs = refs[3:3 + n_lam]
    o_ref = refs[3 + n_lam]
    qc_ref, m_ref, acc_ref, s0, s1, b0, b1, p0, p1, a0, a1 = refs[4 + n_lam:]
    r = _AttnRefs(k_ref, vt_ref, qc_ref, m_ref, acc_ref, (s0, s1), (b0, b1), (p0, p1), (a0, a1))

    assert tq == 2 * tk
    dk = qc_ref.shape[1]
    ncol = heads * n_maps * tq
    qi = pl.program_id(1)
    nblk = 2 * (qi + 1)

    for h in range(heads):
        q = q_ref[:, h * dk:(h + 1) * dk]
        if n_maps == 2:
            lane = lax.broadcasted_iota(jnp.int32, q.shape, 1)
            zero = jnp.zeros_like(q)
            qc_ref[2 * h * tq:(2 * h + 1) * tq, :] = jnp.where(lane < DIFF_HEAD_DIM, q, zero)
            qc_ref[(2 * h + 1) * tq:(2 * h + 2) * tq, :] = jnp.where(lane >= DIFF_HEAD_DIM, q, zero)
        else:
            qc_ref[h * tq:(h + 1) * tq, :] = q

    def key_start(t):
        blk = jnp.where(t < 2, qi * 2 + t, t - 2)
        return pl.multiple_of(blk * tk, tk)

    m_ref[...] = jnp.full(m_ref.shape, MASK_VALUE, jnp.float32)
    acc_ref[...] = jnp.zeros(acc_ref.shape, jnp.float32)

    krow = lax.broadcasted_iota(jnp.int32, (tk, ncol), 0)
    qcol = lax.broadcasted_iota(jnp.int32, (tk, ncol), 1) & (tq - 1)
    masks = [krow + d * tk <= qcol for d in range(2)]

    def stages(score=None, softmax=None, values=()):
        for c0 in range(0, ncol, CHUNK_COLS):
            cols = slice(c0, c0 + CHUNK_COLS)
            head = c0 // (n_maps * tq)
            if softmax is not None:
                _softmax_stage(r, softmax, cols)
            if score is not None:
                _score_stage(r, key_start(score[0]), score[1], score[2], cols, head)
            for t, slot in values:
                _value_stage(r, key_start(t), slot, cols, head)

    def full_step(t, par):
        stages(score=(t, par, None), softmax=1 - par, values=[(t - 2, par)])

    def pair(u, carry):
        full_step(2 * u + 2, 0)
        full_step(2 * u + 3, 1)
        return carry

    stages(score=(0, 0, masks[0]))
    stages(score=(1, 1, masks[1]), softmax=0)
    lax.fori_loop(0, qi, pair, 0)
    stages(softmax=1, values=[(nblk - 2, 0), (nblk - 1, 1)])

    dv = o_ref.shape[1] // heads
    out = acc_ref[:dv, :] * (1.0 / acc_ref[dv:dv + 1, :])
    if n_maps == 2:
        lq1_ref, lk1_ref, lq2_ref, lk2_ref = lam_refs
        lam_init = 0.8 - 0.6 * math.exp(-0.3 * 0)
        lam = (jnp.exp(jnp.sum(lq1_ref[...] * lk1_ref[...], axis=-1, keepdims=True))
               - jnp.exp(jnp.sum(lq2_ref[...] * lk2_ref[...], axis=-1, keepdims=True))
               + lam_init)
    for h in range(heads):
        g = h * n_maps * tq
        oh = out[:, g:g + tq]
        if n_maps == 2:
            oh = oh - lam * out[:, g + tq:g + 2 * tq]
        o_ref[:, h * dv:(h + 1) * dv] = oh.T


def _attention(q, k, vt, lambdas, *, heads, n_maps, dk, name):
    s = q.shape[0]
    n_heads, vt_rows, _ = vt.shape
    dv = vt_rows - SUM_ROWS
    tk, tq = KEY_ROWS, 2 * KEY_ROWS
    ncol = heads * n_maps * tq
    f32, bf = jnp.float32, jnp.bfloat16
    kv_bytes = heads * s * (dk + vt_rows) * 2
    kv_mode = dict(pipeline_mode=pl.Buffered(1)) if 2 * kv_bytes > VMEM_LIMIT_BYTES // 2 else {}
    in_specs = [pl.BlockSpec((tq, heads * dk), lambda h, i: (i, h)),
                pl.BlockSpec((s, heads * dk), lambda h, i: (0, h), **kv_mode),
                pl.BlockSpec((heads, vt_rows, s), lambda h, i: (h, 0, 0), **kv_mode)]
    in_specs += [pl.BlockSpec(lam.shape, lambda h, i: (0, 0)) for lam in lambdas]
    stat = pltpu.VMEM((1, ncol), f32)
    scratch = [pltpu.VMEM((ncol, dk), bf), stat, pltpu.VMEM((vt_rows, ncol), f32),
               pltpu.VMEM((tk, ncol), f32), pltpu.VMEM((tk, ncol), f32), stat, stat,
               pltpu.VMEM((tk, ncol), bf), pltpu.VMEM((tk, ncol), bf), stat, stat]
    return pl.pallas_call(
        functools.partial(_attn_kernel, heads=heads, n_maps=n_maps, tq=tq, tk=tk),
        grid=(n_heads // heads, s // tq),
        in_specs=in_specs,
        out_specs=pl.BlockSpec((tq, heads * dv), lambda h, i: (i, h)),
        out_shape=jax.ShapeDtypeStruct((s, n_heads * dv), jnp.float32),
        scratch_shapes=scratch,
        compiler_params=pltpu.CompilerParams(
            dimension_semantics=("parallel", "parallel"), vmem_limit_bytes=VMEM_LIMIT_BYTES),
        name=name,
    )(q, k, vt, *lambdas)


def _output_kernel(x_ref, od_ref, om_ref, gin_ref, wg_ref, subln_ref, wpd_ref, wpm_ref,
                   wout_ref, gfin_ref, out_ref, *, lam_init):
    x = x_ref[...]
    h = _rms(x, gin_ref[...]).astype(jnp.bfloat16)
    gates = jnp.dot(h, wg_ref[...], preferred_element_type=jnp.float32)
    dgate, mgate = gates[:, 0:512], gates[:, 512:1024]
    g_diff, g_mla = gates[:, 1024:2048], gates[:, 2048:3072]

    od = od_ref[...]
    subln = subln_ref[...]
    od = jnp.concatenate(
        [_rms(od[:, hd * LANES:(hd + 1) * LANES], subln) for hd in range(DIFF_HEADS)], axis=1)
    od = od * (1.0 - lam_init) * jax.nn.silu(dgate)
    om = om_ref[...] * jax.nn.silu(mgate)

    pd = jnp.dot(od.astype(jnp.bfloat16), wpd_ref[...], preferred_element_type=jnp.float32)
    pm = jnp.dot(om.astype(jnp.bfloat16), wpm_ref[...], preferred_element_type=jnp.float32)
    merged = jax.nn.sigmoid(g_diff) * pd + jax.nn.sigmoid(g_mla) * pm
    y = x + jnp.dot(merged.astype(jnp.bfloat16), wout_ref[...],
                    preferred_element_type=jnp.float32)
    out_ref[...] = _rms(y, gfin_ref[...])


def _output_stage(x2, od, om, gin, wg, subln, wpd, wpm, wout, gfin):
    s = x2.shape[0]
    tm = ROW_BLOCK
    full = lambda shape: pl.BlockSpec(shape, lambda i: (0,) * len(shape))
    rows = lambda n: pl.BlockSpec((tm, n), lambda i: (i, 0))
    lam_init = 0.8 - 0.6 * math.exp(-0.3 * 0)
    return pl.pallas_call(
        functools.partial(_output_kernel, lam_init=lam_init),
        grid=(s // tm,),
        in_specs=[rows(D_MODEL), rows(512), rows(512), full(gin.shape), full(wg.shape),
                  full(subln.shape), full(wpd.shape), full(wpm.shape), full(wout.shape),
                  full(gfin.shape)],
        out_specs=rows(D_MODEL),
        out_shape=jax.ShapeDtypeStruct((s, D_MODEL), jnp.float32),
        compiler_params=pltpu.CompilerParams(
            dimension_semantics=("parallel",), vmem_limit_bytes=VMEM_LIMIT_BYTES),
        name="output_stage",
    )(x2, od, om, gin, wg, subln, wpd, wpm, wout, gfin)


def kernel(x, positions, norm_in, w_in, diff_lambda_q1, diff_lambda_k1, diff_lambda_q2,
           diff_lambda_k2, diff_subln, mla_q_norm, w_uq, mla_kv_norm, w_ukv, w_proj_diff,
           w_proj_mla, w_out, norm_final):
    b, s, d = x.shape
    assert b == 1 and d == D_MODEL and w_in.shape[0] == 1
    bf = jnp.bfloat16
    w = w_in[0]
    wa = jnp.concatenate(
        [w[:, 0:1536], w[:, 2048:2752], jnp.zeros((d, 64), w.dtype)], axis=1).astype(bf)
    wg = jnp.concatenate([w[:, 1536:2048], w[:, 2752:5312]], axis=1).astype(bf)
    wuq = w_uq[0].reshape(MLA_Q_RANK, MLA_HEADS, MLA_NOPE + MLA_ROPE)
    wuq = jnp.concatenate(
        [wuq, jnp.zeros((MLA_Q_RANK, MLA_HEADS, 64), wuq.dtype)], axis=2)
    wuq = wuq.reshape(MLA_Q_RANK, MLA_HEADS * 256).astype(bf)
    wukv = w_ukv[0].reshape(MLA_KV_RANK, MLA_HEADS, MLA_NOPE + MLA_V)
    wuk = wukv[:, :, :MLA_NOPE].reshape(MLA_KV_RANK, MLA_HEADS * MLA_NOPE).astype(bf)
    wuv = wukv[:, :, MLA_NOPE:].reshape(MLA_KV_RANK, MLA_HEADS * MLA_V).astype(bf)

    inv_freq = ROPE_THETA ** (-jnp.arange(0, 64, 2, dtype=jnp.float32) / 64)
    invf = jnp.tile(inv_freq, 4).reshape(1, LANES)

    x2 = x.reshape(s, d)
    pos2 = positions.reshape(s, 1)
    qd, kd, vtd, qm, km, vtm = _input_stage(
        x2, pos2, invf, norm_in, wa, mla_q_norm, wuq, mla_kv_norm, wuk, wuv)

    lambdas = (diff_lambda_q1, diff_lambda_k1, diff_lambda_q2, diff_lambda_k2)
    od = _attention(qd, kd, vtd, lambdas, heads=1, n_maps=2, dk=LANES, name="diff_attention")
    om = _attention(qm, km, vtm, (), heads=2, n_maps=1, dk=2 * LANES, name="mla_attention")

    out = _output_stage(x2, od, om, norm_in, wg, diff_subln, w_proj_diff[0].astype(bf),
                        w_proj_mla[0].astype(bf), w_out[0].astype(bf),
                        norm_final.reshape(1, d))
    return out.reshape(b, s, d)
```

```python
import functools
import math

import jax
import jax.numpy as jnp
from jax import lax
from jax.experimental import pallas as pl
from jax.experimental.pallas import tpu as pltpu

D_MODEL = 1024
ROPE_THETA = 10000.0
NORM_EPS = 1e-6

DIFF_HEADS = 4
DIFF_HEAD_DIM = 64
DIFF_V_DIM = 128
MLA_HEADS = 4
MLA_NOPE = 128
MLA_ROPE = 64
MLA_V = 128
MLA_Q_RANK = 384
MLA_KV_RANK = 256

LANES = 128
LOG2E = math.log2(math.e)
MASK_VALUE = -1e30

ROW_BLOCK = 512
KEY_ROWS = 512
CHUNK_COLS = 256
SUM_ROWS = 16
VT_ROWS = MLA_V + SUM_ROWS
VMEM_LIMIT_BYTES = 56 * 1024 * 1024

_A_DQ, _A_DK, _A_DV, _A_CQ, _A_CKV, _A_KR, _A_END = 0, 512, 1024, 1536, 1920, 2176, 2304


def _rms(x, gain):
    return x * lax.rsqrt(jnp.mean(x * x, axis=-1, keepdims=True) + NORM_EPS) * gain


def _rope_tile(t, cos, sin_signed, first_half):
    partner = jnp.where(first_half, pltpu.roll(t, 96, axis=1), pltpu.roll(t, 32, axis=1))
    return t * cos + partner * sin_signed


def _rope_tables(pos, invf, lane_group):
    rows = pos.shape[0]
    quarter = rows // 4
    packed = pos[3 * quarter:, :]
    for j in (2, 1, 0):
        packed = jnp.where(lane_group == j, pos[j * quarter:(j + 1) * quarter, :], packed)
    ang = packed * invf

    def spread(tab):
        rolled = [tab] + [pltpu.roll(tab, 32 * k, axis=1) for k in (1, 2, 3)]
        parts = []
        for j in range(4):
            part = rolled[(3 - j) % 4]
            for g in (2, 1, 0):
                part = jnp.where(lane_group == g, rolled[(g - j) % 4], part)
            parts.append(part)
        return jnp.concatenate(parts, axis=0)

    return spread(jnp.cos(ang)), spread(jnp.sin(ang))


def _input_kernel(x_ref, pos_ref, invf_ref, gin_ref, wa_ref, gq_ref, wuq_ref, gkv_ref,
                  wuk_ref, wuv_ref,
                  qd_ref, kd_ref, vtd_ref, qm_ref, km_ref, vtm_ref,
                  *, diff_scale, mla_scale):
    x = x_ref[...]
    h = _rms(x, gin_ref[...]).astype(jnp.bfloat16)
    proj = jnp.dot(h, wa_ref[...], preferred_element_type=jnp.float32)

    lane = lax.broadcasted_iota(jnp.int32, (1, LANES), 1)
    first_half = (lane & 63) < 32
    cos, sin = _rope_tables(pos_ref[...].astype(jnp.float32), invf_ref[...], lane >> 5)
    sin_signed = sin * jnp.where(first_half, -1.0, 1.0)
    rope = functools.partial(_rope_tile, cos=cos, sin_signed=sin_signed, first_half=first_half)
    ones = jnp.ones((SUM_ROWS, x.shape[0]), jnp.bfloat16)

    for hd in range(DIFF_HEADS):
        c = hd * LANES
        q = rope(proj[:, _A_DQ + c:_A_DQ + c + LANES]) * diff_scale
        qd_ref[:, c:c + LANES] = q.astype(jnp.bfloat16)
        k = rope(proj[:, _A_DK + c:_A_DK + c + LANES])
        kd_ref[:, c:c + LANES] = k.astype(jnp.bfloat16)
        v = proj[:, _A_DV + c:_A_DV + c + LANES]
        vtd_ref[hd, :DIFF_V_DIM, :] = v.T.astype(jnp.bfloat16)
        vtd_ref[hd, DIFF_V_DIM:, :] = ones

    cq = _rms(proj[:, _A_CQ:_A_CKV], gq_ref[...]).astype(jnp.bfloat16)
    qm = jnp.dot(cq, wuq_ref[...], preferred_element_type=jnp.float32)
    ckv = _rms(proj[:, _A_CKV:_A_KR], gkv_ref[...]).astype(jnp.bfloat16)
    kn = jnp.dot(ckv, wuk_ref[...], preferred_element_type=jnp.float32)
    mv = jnp.dot(ckv, wuv_ref[...], preferred_element_type=jnp.float32)
    kr = rope(proj[:, _A_KR:_A_END]).astype(jnp.bfloat16)

    for hd in range(MLA_HEADS):
        c = hd * 2 * LANES
        qm_ref[:, c:c + LANES] = (qm[:, c:c + LANES] * mla_scale).astype(jnp.bfloat16)
        qr = rope(qm[:, c + LANES:c + 2 * LANES]) * mla_scale
        qm_ref[:, c + LANES:c + 2 * LANES] = qr.astype(jnp.bfloat16)
        km_ref[:, c:c + LANES] = kn[:, hd * LANES:(hd + 1) * LANES].astype(jnp.bfloat16)
        km_ref[:, c + LANES:c + 2 * LANES] = kr
        vtm_ref[hd, :MLA_V, :] = mv[:, hd * LANES:(hd + 1) * LANES].T.astype(jnp.bfloat16)
        vtm_ref[hd, MLA_V:, :] = ones


def _input_stage(x2, pos2, invf, gin, wa, gq, wuq, gkv, wuk, wuv):
    s = x2.shape[0]
    tm = ROW_BLOCK
    full = lambda shape: pl.BlockSpec(shape, lambda i: (0,) * len(shape))
    rows = lambda n: pl.BlockSpec((tm, n), lambda i: (i, 0))
    vt_spec = pl.BlockSpec((DIFF_HEADS, VT_ROWS, tm), lambda i: (0, 0, i))
    kern = functools.partial(
        _input_kernel,
        diff_scale=DIFF_HEAD_DIM ** -0.5 * LOG2E,
        mla_scale=(MLA_NOPE + MLA_ROPE) ** -0.5 * LOG2E)
    bf = jnp.bfloat16
    return pl.pallas_call(
        kern,
        grid=(s // tm,),
        in_specs=[rows(D_MODEL), rows(1), full(invf.shape), full(gin.shape), full(wa.shape),
                  full(gq.shape), full(wuq.shape), full(gkv.shape), full(wuk.shape),
                  full(wuv.shape)],
        out_specs=[rows(512), rows(512), vt_spec, rows(1024), rows(1024), vt_spec],
        out_shape=[jax.ShapeDtypeStruct((s, 512), bf), jax.ShapeDtypeStruct((s, 512), bf),
                   jax.ShapeDtypeStruct((DIFF_HEADS, VT_ROWS, s), bf),
                   jax.ShapeDtypeStruct((s, 1024), bf), jax.ShapeDtypeStruct((s, 1024), bf),
                   jax.ShapeDtypeStruct((MLA_HEADS, VT_ROWS, s), bf)],
        compiler_params=pltpu.CompilerParams(
            dimension_semantics=("parallel",), vmem_limit_bytes=VMEM_LIMIT_BYTES),
        name="input_stage",
    )(x2, pos2, invf, gin, wa, gq, wuq, gkv, wuk, wuv)


class _AttnRefs:
    def __init__(self, k_ref, vt_ref, qc_ref, m_ref, acc_ref, s_refs, bmax_refs, p_refs,
                 alpha_refs):
        self.k, self.vt, self.qc, self.m, self.acc = k_ref, vt_ref, qc_ref, m_ref, acc_ref
        self.s, self.bmax, self.p, self.alpha = s_refs, bmax_refs, p_refs, alpha_refs


def _score_stage(r, start, slot, mask, cols, head):
    dk = r.qc.shape[0]
    kb = r.k[pl.ds(start, r.s[slot].shape[0]), head * dk:(head + 1) * dk]
    st = jnp.dot(kb, r.qc[:, cols], preferred_element_type=jnp.float32)
    if mask is not None:
        st = jnp.where(mask[:, cols], st, MASK_VALUE)
    r.s[slot][:, cols] = st
    r.bmax[slot][:, cols] = jnp.max(st, axis=0, keepdims=True)


def _softmax_stage(r, slot, cols):
    m_old = r.m[:, cols]
    m_new = jnp.maximum(m_old, r.bmax[slot][:, cols])
    r.alpha[slot][:, cols] = jnp.exp2(m_old - m_new)
    r.m[:, cols] = m_new
    r.p[slot][:, cols] = jnp.exp2(r.s[slot][:, cols] - m_new).astype(jnp.bfloat16)


def _value_stage(r, start, slot, cols, head):
    vtb = r.vt[head, :, pl.ds(start, r.p[slot].shape[0])]
    pv = jnp.dot(vtb, r.p[slot][:, cols], preferred_element_type=jnp.float32)
    r.acc[:, cols] = r.alpha[slot][:, cols] * r.acc[:, cols] + pv


def _attn_kernel(*refs, heads, n_maps, tq, tk):
    n_lam = 4 if n_maps == 2 else 0
    q_ref, k_ref, vt_ref = refs[:3]
    lam_refs = refs[3:3 + n_lam]
    o_ref = refs[3 + n_lam]
    qc_ref, m_ref, acc_ref, s0, s1, b0, b1, p0, p1, a0, a1 = refs[4 + n_lam:]
    r = _AttnRefs(k_ref, vt_ref, qc_ref, m_ref, acc_ref, (s0, s1), (b0, b1), (p0, p1), (a0, a1))

    assert tq == 2 * tk
    dk = qc_ref.shape[0]
    ncol = heads * n_maps * tq
    qi = pl.program_id(1)
    nblk = 2 * (qi + 1)

    for h in range(heads):
        qt = q_ref[:, h * dk:(h + 1) * dk].astype(jnp.float32).T
        if n_maps == 2:
            feat = lax.broadcasted_iota(jnp.int32, qt.shape, 0)
            zero = jnp.zeros_like(qt)
            qc_ref[:, 2 * h * tq:(2 * h + 1) * tq] = jnp.where(
                feat < DIFF_HEAD_DIM, qt, zero).astype(jnp.bfloat16)
            qc_ref[:, (2 * h + 1) * tq:(2 * h + 2) * tq] = jnp.where(
                feat >= DIFF_HEAD_DIM, qt, zero).astype(jnp.bfloat16)
        else:
            qc_ref[:, h * tq:(h + 1) * tq] = qt.astype(jnp.bfloat16)

    def key_start(t):
        blk = jnp.where(t < 2, qi * 2 + t, t - 2)
        return pl.multiple_of(blk * tk, tk)

    m_ref[...] = jnp.full(m_ref.shape, MASK_VALUE, jnp.float32)
    acc_ref[...] = jnp.zeros(acc_ref.shape, jnp.float32)

    krow = lax.broadcasted_iota(jnp.int32, (tk, ncol), 0)
    qcol = lax.broadcasted_iota(jnp.int32, (tk, ncol), 1) & (tq - 1)
    masks = [krow + d * tk <= qcol for d in range(2)]

    def stages(score=None, softmax=None, values=()):
        for c0 in range(0, ncol, CHUNK_COLS):
            cols = slice(c0, c0 + CHUNK_COLS)
            head = c0 // (n_maps * tq)
            if softmax is not None:
                _softmax_stage(r, softmax, cols)
            if score is not None:
                _score_stage(r, key_start(score[0]), score[1], score[2], cols, head)
            for t, slot in values:
                _value_stage(r, key_start(t), slot, cols, head)

    def full_step(t, par):
        stages(score=(t, par, None), softmax=1 - par, values=[(t - 2, par)])

    def pair(u, carry):
        full_step(2 * u + 2, 0)
        full_step(2 * u + 3, 1)
        return carry

    stages(score=(0, 0, masks[0]))
    stages(score=(1, 1, masks[1]), softmax=0)
    lax.fori_loop(0, qi, pair, 0)
    stages(softmax=1, values=[(nblk - 2, 0), (nblk - 1, 1)])

    dv = o_ref.shape[1] // heads
    out = acc_ref[:dv, :] * (1.0 / acc_ref[dv:dv + 1, :])
    if n_maps == 2:
        lq1_ref, lk1_ref, lq2_ref, lk2_ref = lam_refs
        lam_init = 0.8 - 0.6 * math.exp(-0.3 * 0)
        lam = (jnp.exp(jnp.sum(lq1_ref[...] * lk1_ref[...], axis=-1, keepdims=True))
               - jnp.exp(jnp.sum(lq2_ref[...] * lk2_ref[...], axis=-1, keepdims=True))
               + lam_init)
    for h in range(heads):
        g = h * n_maps * tq
        oh = out[:, g:g + tq]
        if n_maps == 2:
            oh = oh - lam * out[:, g + tq:g + 2 * tq]
        o_ref[:, h * dv:(h + 1) * dv] = oh.T


def _attention(q, k, vt, lambdas, *, heads, n_maps, dk, name):
    s = q.shape[0]
    n_heads, vt_rows, _ = vt.shape
    dv = vt_rows - SUM_ROWS
    tk, tq = KEY_ROWS, 2 * KEY_ROWS
    ncol = heads * n_maps * tq
    f32, bf = jnp.float32, jnp.bfloat16
    kv_bytes = heads * s * (dk + vt_rows) * 2
    kv_mode = dict(pipeline_mode=pl.Buffered(1)) if 2 * kv_bytes > VMEM_LIMIT_BYTES // 2 else {}
    in_specs = [pl.BlockSpec((tq, heads * dk), lambda h, i: (i, h)),
                pl.BlockSpec((s, heads * dk), lambda h, i: (0, h), **kv_mode),
                pl.BlockSpec((heads, vt_rows, s), lambda h, i: (h, 0, 0), **kv_mode)]
    in_specs += [pl.BlockSpec(lam.shape, lambda h, i: (0, 0)) for lam in lambdas]
    stat = pltpu.VMEM((1, ncol), f32)
    scratch = [pltpu.VMEM((dk, ncol), bf), stat, pltpu.VMEM((vt_rows, ncol), f32),
               pltpu.VMEM((tk, ncol), f32), pltpu.VMEM((tk, ncol), f32), stat, stat,
               pltpu.VMEM((tk, ncol), bf), pltpu.VMEM((tk, ncol), bf), stat, stat]
    return pl.pallas_call(
        functools.partial(_attn_kernel, heads=heads, n_maps=n_maps, tq=tq, tk=tk),
        grid=(n_heads // heads, s // tq),
        in_specs=in_specs,
        out_specs=pl.BlockSpec((tq, heads * dv), lambda h, i: (i, h)),
        out_shape=jax.ShapeDtypeStruct((s, n_heads * dv), jnp.float32),
        scratch_shapes=scratch,
        compiler_params=pltpu.CompilerParams(
            dimension_semantics=("parallel", "parallel"), vmem_limit_bytes=VMEM_LIMIT_BYTES),
        name=name,
    )(q, k, vt, *lambdas)


def _output_kernel(x_ref, od_ref, om_ref, gin_ref, wg_ref, subln_ref, wpd_ref, wpm_ref,
                   wout_ref, gfin_ref, out_ref, *, lam_init):
    x = x_ref[...]
    h = _rms(x, gin_ref[...]).astype(jnp.bfloat16)
    gates = jnp.dot(h, wg_ref[...], preferred_element_type=jnp.float32)
    dgate, mgate = gates[:, 0:512], gates[:, 512:1024]
    g_diff, g_mla = gates[:, 1024:2048], gates[:, 2048:3072]

    od = od_ref[...]
    subln = subln_ref[...]
    od = jnp.concatenate(
        [_rms(od[:, hd * LANES:(hd + 1) * LANES], subln) for hd in range(DIFF_HEADS)], axis=1)
    od = od * (1.0 - lam_init) * jax.nn.silu(dgate)
    om = om_ref[...] * jax.nn.silu(mgate)

    pd = jnp.dot(od.astype(jnp.bfloat16), wpd_ref[...], preferred_element_type=jnp.float32)
    pm = jnp.dot(om.astype(jnp.bfloat16), wpm_ref[...], preferred_element_type=jnp.float32)
    merged = jax.nn.sigmoid(g_diff) * pd + jax.nn.sigmoid(g_mla) * pm
    y = x + jnp.dot(merged.astype(jnp.bfloat16), wout_ref[...],
                    preferred_element_type=jnp.float32)
    out_ref[...] = _rms(y, gfin_ref[...])


def _output_stage(x2, od, om, gin, wg, subln, wpd, wpm, wout, gfin):
    s = x2.shape[0]
    tm = ROW_BLOCK
    full = lambda shape: pl.BlockSpec(shape, lambda i: (0,) * len(shape))
    rows = lambda n: pl.BlockSpec((tm, n), lambda i: (i, 0))
    lam_init = 0.8 - 0.6 * math.exp(-0.3 * 0)
    return pl.pallas_call(
        functools.partial(_output_kernel, lam_init=lam_init),
        grid=(s // tm,),
        in_specs=[rows(D_MODEL), rows(512), rows(512), full(gin.shape), full(wg.shape),
                  full(subln.shape), full(wpd.shape), full(wpm.shape), full(wout.shape),
                  full(gfin.shape)],
        out_specs=rows(D_MODEL),
        out_shape=jax.ShapeDtypeStruct((s, D_MODEL), jnp.float32),
        compiler_params=pltpu.CompilerParams(
            dimension_semantics=("parallel",), vmem_limit_bytes=VMEM_LIMIT_BYTES),
        name="output_stage",
    )(x2, od, om, gin, wg, subln, wpd, wpm, wout, gfin)


def kernel(x, positions, norm_in, w_in, diff_lambda_q1, diff_lambda_k1, diff_lambda_q2,
           diff_lambda_k2, diff_subln, mla_q_norm, w_uq, mla_kv_norm, w_ukv, w_proj_diff,
           w_proj_mla, w_out, norm_final):
    b, s, d = x.shape
    assert b == 1 and d == D_MODEL and w_in.shape[0] == 1
    bf = jnp.bfloat16
    w = w_in[0]
    wa = jnp.concatenate(
        [w[:, 0:1536], w[:, 2048:2752], jnp.zeros((d, 64), w.dtype)], axis=1).astype(bf)
    wg = jnp.concatenate([w[:, 1536:2048], w[:, 2752:5312]], axis=1).astype(bf)
    wuq = w_uq[0].reshape(MLA_Q_RANK, MLA_HEADS, MLA_NOPE + MLA_ROPE)
    wuq = jnp.concatenate(
        [wuq, jnp.zeros((MLA_Q_RANK, MLA_HEADS, 64), wuq.dtype)], axis=2)
    wuq = wuq.reshape(MLA_Q_RANK, MLA_HEADS * 256).astype(bf)
    wukv = w_ukv[0].reshape(MLA_KV_RANK, MLA_HEADS, MLA_NOPE + MLA_V)
    wuk = wukv[:, :, :MLA_NOPE].reshape(MLA_KV_RANK, MLA_HEADS * MLA_NOPE).astype(bf)
    wuv = wukv[:, :, MLA_NOPE:].reshape(MLA_KV_RANK, MLA_HEADS * MLA_V).astype(bf)

    inv_freq = ROPE_THETA ** (-jnp.arange(0, 64, 2, dtype=jnp.float32) / 64)
    invf = jnp.tile(inv_freq, 4).reshape(1, LANES)

    x2 = x.reshape(s, d)
    pos2 = positions.reshape(s, 1)
    qd, kd, vtd, qm, km, vtm = _input_stage(
        x2, pos2, invf, norm_in, wa, mla_q_norm, wuq, mla_kv_norm, wuk, wuv)

    lambdas = (diff_lambda_q1, diff_lambda_k1, diff_lambda_q2, diff_lambda_k2)
    od = _attention(qd, kd, vtd, lambdas, heads=1, n_maps=2, dk=LANES, name="diff_attention")
    om = _attention(qm, km, vtm, (), heads=2, n_maps=1, dk=2 * LANES, name="mla_attention")

    out = _output_stage(x2, od, om, norm_in, wg, diff_subln, w_proj_diff[0].astype(bf),
                        w_proj_mla[0].astype(bf), w_out[0].astype(bf),
                        norm_final.reshape(1, d))
    return out.reshape(b, s, d)
```

```python
import functools
import math

import jax
import jax.numpy as jnp
from jax import lax
from jax.experimental import pallas as pl
from jax.experimental.pallas import tpu as pltpu

D_MODEL = 1024
ROPE_THETA = 10000.0
NORM_EPS = 1e-6

DIFF_HEADS = 4
DIFF_HEAD_DIM = 64
DIFF_V_DIM = 128
MLA_HEADS = 4
MLA_NOPE = 128
MLA_ROPE = 64
MLA_V = 128
MLA_Q_RANK = 384
MLA_KV_RANK = 256

LANES = 128
LOG2E = math.log2(math.e)
MASK_VALUE = -1e30

ROW_BLOCK = 512
KEY_ROWS = 512
CHUNK_COLS = 256
SUM_ROWS = 16
VT_ROWS = MLA_V + SUM_ROWS
VMEM_LIMIT_BYTES = 56 * 1024 * 1024

_A_DQ, _A_DK, _A_DV, _A_CQ, _A_CKV, _A_KR, _A_END = 0, 512, 1024, 1536, 1920, 2176, 2304


def _rms(x, gain):
    return x * lax.rsqrt(jnp.mean(x * x, axis=-1, keepdims=True) + NORM_EPS) * gain


def _rope_tile(t, cos, sin_signed, first_half):
    partner = jnp.where(first_half, pltpu.roll(t, 96, axis=1), pltpu.roll(t, 32, axis=1))
    return t * cos + partner * sin_signed


def _rope_tables(pos, invf, lane_group):
    rows = pos.shape[0]
    quarter = rows // 4
    packed = pos[3 * quarter:, :]
    for j in (2, 1, 0):
        packed = jnp.where(lane_group == j, pos[j * quarter:(j + 1) * quarter, :], packed)
    ang = packed * invf

    def spread(tab):
        rolled = [tab] + [pltpu.roll(tab, 32 * k, axis=1) for k in (1, 2, 3)]
        parts = []
        for j in range(4):
            part = rolled[(3 - j) % 4]
            for g in (2, 1, 0):
                part = jnp.where(lane_group == g, rolled[(g - j) % 4], part)
            parts.append(part)
        return jnp.concatenate(parts, axis=0)

    return spread(jnp.cos(ang)), spread(jnp.sin(ang))


def _input_kernel(x_ref, pos_ref, invf_ref, gin_ref, wa_ref, gq_ref, wuq_ref, gkv_ref,
                  wuk_ref, wuv_ref,
                  qd_ref, kd_ref, vtd_ref, qm_ref, km_ref, vtm_ref,
                  *, diff_scale, mla_scale):
    x = x_ref[...]
    h = _rms(x, gin_ref[...]).astype(jnp.bfloat16)
    proj = jnp.dot(h, wa_ref[...], preferred_element_type=jnp.float32)

    lane = lax.broadcasted_iota(jnp.int32, (1, LANES), 1)
    first_half = (lane & 63) < 32
    cos, sin = _rope_tables(pos_ref[...].astype(jnp.float32), invf_ref[...], lane >> 5)
    sin_signed = sin * jnp.where(first_half, -1.0, 1.0)
    rope = functools.partial(_rope_tile, cos=cos, sin_signed=sin_signed, first_half=first_half)
    ones = jnp.ones((SUM_ROWS, x.shape[0]), jnp.bfloat16)

    for hd in range(DIFF_HEADS):
        c = hd * LANES
        q = rope(proj[:, _A_DQ + c:_A_DQ + c + LANES]) * diff_scale
        qd_ref[:, c:c + LANES] = q.astype(jnp.bfloat16)
        k = rope(proj[:, _A_DK + c:_A_DK + c + LANES])
        kd_ref[:, c:c + LANES] = k.astype(jnp.bfloat16)
        v = proj[:, _A_DV + c:_A_DV + c + LANES]
        vtd_ref[hd, :DIFF_V_DIM, :] = v.T.astype(jnp.bfloat16)
        vtd_ref[hd, DIFF_V_DIM:, :] = ones

    cq = _rms(proj[:, _A_CQ:_A_CKV], gq_ref[...]).astype(jnp.bfloat16)
    qm = jnp.dot(cq, wuq_ref[...], preferred_element_type=jnp.float32)
    ckv = _rms(proj[:, _A_CKV:_A_KR], gkv_ref[...]).astype(jnp.bfloat16)
    kn = jnp.dot(ckv, wuk_ref[...], preferred_element_type=jnp.float32)
    mv = jnp.dot(ckv, wuv_ref[...], preferred_element_type=jnp.float32)
    kr = rope(proj[:, _A_KR:_A_END]).astype(jnp.bfloat16)

    for hd in range(MLA_HEADS):
        c = hd * 2 * LANES
        qm_ref[:, c:c + LANES] = (qm[:, c:c + LANES] * mla_scale).astype(jnp.bfloat16)
        qr = rope(qm[:, c + LANES:c + 2 * LANES]) * mla_scale
        qm_ref[:, c + LANES:c + 2 * LANES] = qr.astype(jnp.bfloat16)
        km_ref[:, c:c + LANES] = kn[:, hd * LANES:(hd + 1) * LANES].astype(jnp.bfloat16)
        km_ref[:, c + LANES:c + 2 * LANES] = kr
        vtm_ref[hd, :MLA_V, :] = mv[:, hd * LANES:(hd + 1) * LANES].T.astype(jnp.bfloat16)
        vtm_ref[hd, MLA_V:, :] = ones


def _input_stage(x2, pos2, invf, gin, wa, gq, wuq, gkv, wuk, wuv):
    s = x2.shape[0]
    tm = ROW_BLOCK
    full = lambda shape: pl.BlockSpec(shape, lambda i: (0,) * len(shape))
    rows = lambda n: pl.BlockSpec((tm, n), lambda i: (i, 0))
    vt_spec = pl.BlockSpec((DIFF_HEADS, VT_ROWS, tm), lambda i: (0, 0, i))
    kern = functools.partial(
        _input_kernel,
        diff_scale=DIFF_HEAD_DIM ** -0.5 * LOG2E,
        mla_scale=(MLA_NOPE + MLA_ROPE) ** -0.5 * LOG2E)
    bf = jnp.bfloat16
    return pl.pallas_call(
        kern,
        grid=(s // tm,),
        in_specs=[rows(D_MODEL), rows(1), full(invf.shape), full(gin.shape), full(wa.shape),
                  full(gq.shape), full(wuq.shape), full(gkv.shape), full(wuk.shape),
                  full(wuv.shape)],
        out_specs=[rows(512), rows(512), vt_spec, rows(1024), rows(1024), vt_spec],
        out_shape=[jax.ShapeDtypeStruct((s, 512), bf), jax.ShapeDtypeStruct((s, 512), bf),
                   jax.ShapeDtypeStruct((DIFF_HEADS, VT_ROWS, s), bf),
                   jax.ShapeDtypeStruct((s, 1024), bf), jax.ShapeDtypeStruct((s, 1024), bf),
                   jax.ShapeDtypeStruct((MLA_HEADS, VT_ROWS, s), bf)],
        compiler_params=pltpu.CompilerParams(
            dimension_semantics=("parallel",), vmem_limit_bytes=VMEM_LIMIT_BYTES),
        name="input_stage",
    )(x2, pos2, invf, gin, wa, gq, wuq, gkv, wuk, wuv)


class _AttnRefs:
    def __init__(self, k_ref, vt_ref, qc_ref, m_ref, acc_ref, s_refs, bmax_refs, p_refs,
                 alpha_refs):
        self.k, self.vt, self.qc, self.m, self.acc = k_ref, vt_ref, qc_ref, m_ref, acc_ref
        self.s, self.bmax, self.p, self.alpha = s_refs, bmax_refs, p_refs, alpha_refs


def _score_stage(r, start, slot, diag, cols, head, tq):
    dk = r.qc.shape[0]
    tk = r.s[slot].shape[0]
    ncols = cols.stop - cols.start
    j0 = cols.start % tq
    if diag is not None and diag * tk >= j0 + ncols:
        r.s[slot][:, cols] = jnp.full((tk, ncols), MASK_VALUE, jnp.float32)
        r.bmax[slot][:, cols] = jnp.full((1, ncols), MASK_VALUE, jnp.float32)
        return
    kb = r.k[pl.ds(start, tk), head * dk:(head + 1) * dk]
    st = jnp.dot(kb, r.qc[:, cols], preferred_element_type=jnp.float32)
    if diag is not None and diag * tk + tk - 1 > j0:
        i = lax.broadcasted_iota(jnp.int32, (tk, ncols), 0) + diag * tk
        j = lax.broadcasted_iota(jnp.int32, (tk, ncols), 1) + j0
        st = jnp.where(i <= j, st, MASK_VALUE)
    r.s[slot][:, cols] = st
    r.bmax[slot][:, cols] = jnp.max(st, axis=0, keepdims=True)


def _softmax_stage(r, slot, cols, first=False):
    m_old = jnp.full((1, cols.stop - cols.start), MASK_VALUE, jnp.float32) if first else r.m[:, cols]
    m_new = jnp.maximum(m_old, r.bmax[slot][:, cols])
    r.alpha[slot][:, cols] = jnp.exp2(m_old - m_new)
    r.m[:, cols] = m_new
    r.p[slot][:, cols] = jnp.exp2(r.s[slot][:, cols] - m_new).astype(jnp.bfloat16)


def _value_stage(r, start, slot, cols, head):
    vtb = r.vt[head, :, pl.ds(start, r.p[slot].shape[0])]
    pv = jnp.dot(vtb, r.p[slot][:, cols], preferred_element_type=jnp.float32)
    r.acc[:, cols] = r.alpha[slot][:, cols] * r.acc[:, cols] + pv


def _attn_kernel(*refs, heads, n_maps, tq, tk, nq):
    n_lam = 4 if n_maps == 2 else 0
    q_ref, k_ref, vt_ref = refs[:3]
    lam_refs = refs[3:3 + n_lam]
    o_ref = refs[3 + n_lam]
    qc_ref, m_ref, acc_ref, s0, s1, b0, b1, p0, p1, a0, a1 = refs[4 + n_lam:]
    r = _AttnRefs(k_ref, vt_ref, qc_ref, m_ref, acc_ref, (s0, s1), (b0, b1), (p0, p1), (a0, a1))

    assert tq == 2 * tk
    dk = qc_ref.shape[0]
    ncol = heads * n_maps * tq
    qi = pl.program_id(1)
    prev = jnp.maximum(qi - 1, 0)

    def key_start(b, t):
        blk = jnp.where(t < 2, b * 2 + t, t - 2)
        return pl.multiple_of(blk * tk, tk)

    def stages(score=None, softmax=None, values=(), first=False):
        for c0 in range(0, ncol, CHUNK_COLS):
            cols = slice(c0, c0 + CHUNK_COLS)
            head = c0 // (n_maps * tq)
            if softmax is not None:
                _softmax_stage(r, softmax, cols, first)
            if score is not None:
                _score_stage(r, score[0], score[1], score[2], cols, head, tq)
            for start, slot in values:
                _value_stage(r, start, slot, cols, head)

    def finish_prev_block():
        dv = o_ref.shape[1] // heads
        out = acc_ref[:dv, :] * (1.0 / acc_ref[dv:dv + 1, :])
        if n_maps == 2:
            lq1_ref, lk1_ref, lq2_ref, lk2_ref = lam_refs
            lam_init = 0.8 - 0.6 * math.exp(-0.3 * 0)
            lam = (jnp.exp(jnp.sum(lq1_ref[...] * lk1_ref[...], axis=-1, keepdims=True))
                   - jnp.exp(jnp.sum(lq2_ref[...] * lk2_ref[...], axis=-1, keepdims=True))
                   + lam_init)
        for h in range(heads):
            g = h * n_maps * tq
            oh = out[:, g:g + tq]
            if n_maps == 2:
                oh = oh - lam * out[:, g + tq:g + 2 * tq]
            o_ref[:, h * dv:(h + 1) * dv] = oh.T

    prev_a = key_start(prev, 2 * prev)
    prev_b = key_start(prev, 2 * prev + 1)

    @pl.when(qi == 0)
    def _():
        for ref in (s1, b1, p0, p1, a0, a1, m_ref, acc_ref):
            ref[...] = jnp.zeros(ref.shape, ref.dtype)

    @pl.when(qi < nq)
    def _():
        for h in range(heads):
            qt = q_ref[:, h * dk:(h + 1) * dk].astype(jnp.float32).T
            if n_maps == 2:
                feat = lax.broadcasted_iota(jnp.int32, qt.shape, 0)
                zero = jnp.zeros_like(qt)
                qc_ref[:, 2 * h * tq:(2 * h + 1) * tq] = jnp.where(
                    feat < DIFF_HEAD_DIM, qt, zero).astype(jnp.bfloat16)
                qc_ref[:, (2 * h + 1) * tq:(2 * h + 2) * tq] = jnp.where(
                    feat >= DIFF_HEAD_DIM, qt, zero).astype(jnp.bfloat16)
            else:
                qc_ref[:, h * tq:(h + 1) * tq] = qt.astype(jnp.bfloat16)

        stages(score=(key_start(qi, 0), 0, 0), softmax=1, values=[(prev_a, 0)])
        stages(score=(key_start(qi, 1), 1, 1), softmax=0, values=[(prev_b, 1)], first=True)
        pl.when(qi > 0)(finish_prev_block)

        def pair(u, carry):
            t = 2 * u
            stages(score=(key_start(qi, t), 0, None), softmax=1,
                   values=[(key_start(qi, t - 2), 0)])
            stages(score=(key_start(qi, t + 1), 1, None), softmax=0,
                   values=[(key_start(qi, t - 1), 1)])
            return carry

        lax.fori_loop(1, qi + 1, pair, 0)

    @pl.when(qi == nq)
    def _():
        stages(softmax=1, values=[(prev_a, 0), (prev_b, 1)])
        finish_prev_block()


def _attention(q, k, vt, lambdas, *, heads, n_maps, dk, name):
    s = q.shape[0]
    n_heads, vt_rows, _ = vt.shape
    dv = vt_rows - SUM_ROWS
    tk, tq = KEY_ROWS, 2 * KEY_ROWS
    nq = s // tq
    ncol = heads * n_maps * tq
    f32, bf = jnp.float32, jnp.bfloat16
    kv_bytes = heads * s * (dk + vt_rows) * 2
    kv_mode = dict(pipeline_mode=pl.Buffered(1)) if 2 * kv_bytes > VMEM_LIMIT_BYTES // 2 else {}
    in_specs = [pl.BlockSpec((tq, heads * dk), lambda h, i: (jnp.minimum(i, nq - 1), h)),
                pl.BlockSpec((s, heads * dk), lambda h, i: (0, h), **kv_mode),
                pl.BlockSpec((heads, vt_rows, s), lambda h, i: (h, 0, 0), **kv_mode)]
    in_specs += [pl.BlockSpec(lam.shape, lambda h, i: (0, 0)) for lam in lambdas]
    stat = pltpu.VMEM((1, ncol), f32)
    scratch = [pltpu.VMEM((dk, ncol), bf), stat, pltpu.VMEM((vt_rows, ncol), f32),
               pltpu.VMEM((tk, ncol), f32), pltpu.VMEM((tk, ncol), f32), stat, stat,
               pltpu.VMEM((tk, ncol), bf), pltpu.VMEM((tk, ncol), bf), stat, stat]
    return pl.pallas_call(
        functools.partial(_attn_kernel, heads=heads, n_maps=n_maps, tq=tq, tk=tk, nq=nq),
        grid=(n_heads // heads, nq + 1),
        in_specs=in_specs,
        out_specs=pl.BlockSpec((tq, heads * dv), lambda h, i: (jnp.maximum(i - 1, 0), h)),
        out_shape=jax.ShapeDtypeStruct((s, n_heads * dv), jnp.float32),
        scratch_shapes=scratch,
        compiler_params=pltpu.CompilerParams(
            dimension_semantics=("arbitrary", "arbitrary"), vmem_limit_bytes=VMEM_LIMIT_BYTES),
        name=name,
    )(q, k, vt, *lambdas)


def _output_kernel(x_ref, od_ref, om_ref, gin_ref, wg_ref, subln_ref, wpd_ref, wpm_ref,
                   wout_ref, gfin_ref, out_ref, *, lam_init):
    x = x_ref[...]
    h = _rms(x, gin_ref[...]).astype(jnp.bfloat16)
    gates = jnp.dot(h, wg_ref[...], preferred_element_type=jnp.float32)
    dgate, mgate = gates[:, 0:512], gates[:, 512:1024]
    g_diff, g_mla = gates[:, 1024:2048], gates[:, 2048:3072]

    od = od_ref[...]
    subln = subln_ref[...]
    od = jnp.concatenate(
        [_rms(od[:, hd * LANES:(hd + 1) * LANES], subln) for hd in range(DIFF_HEADS)], axis=1)
    od = od * (1.0 - lam_init) * jax.nn.silu(dgate)
    om = om_ref[...] * jax.nn.silu(mgate)

    pd = jnp.dot(od.astype(jnp.bfloat16), wpd_ref[...], preferred_element_type=jnp.float32)
    pm = jnp.dot(om.astype(jnp.bfloat16), wpm_ref[...], preferred_element_type=jnp.float32)
    merged = jax.nn.sigmoid(g_diff) * pd + jax.nn.sigmoid(g_mla) * pm
    y = x + jnp.dot(merged.astype(jnp.bfloat16), wout_ref[...],
                    preferred_element_type=jnp.float32)
    out_ref[...] = _rms(y, gfin_ref[...])


def _output_stage(x2, od, om, gin, wg, subln, wpd, wpm, wout, gfin):
    s = x2.shape[0]
    tm = ROW_BLOCK
    full = lambda shape: pl.BlockSpec(shape, lambda i: (0,) * len(shape))
    rows = lambda n: pl.BlockSpec((tm, n), lambda i: (i, 0))
    lam_init = 0.8 - 0.6 * math.exp(-0.3 * 0)
    return pl.pallas_call(
        functools.partial(_output_kernel, lam_init=lam_init),
        grid=(s // tm,),
        in_specs=[rows(D_MODEL), rows(512), rows(512), full(gin.shape), full(wg.shape),
                  full(subln.shape), full(wpd.shape), full(wpm.shape), full(wout.shape),
                  full(gfin.shape)],
        out_specs=rows(D_MODEL),
        out_shape=jax.ShapeDtypeStruct((s, D_MODEL), jnp.float32),
        compiler_params=pltpu.CompilerParams(
            dimension_semantics=("parallel",), vmem_limit_bytes=VMEM_LIMIT_BYTES),
        name="output_stage",
    )(x2, od, om, gin, wg, subln, wpd, wpm, wout, gfin)


def kernel(x, positions, norm_in, w_in, diff_lambda_q1, diff_lambda_k1, diff_lambda_q2,
           diff_lambda_k2, diff_subln, mla_q_norm, w_uq, mla_kv_norm, w_ukv, w_proj_diff,
           w_proj_mla, w_out, norm_final):
    b, s, d = x.shape
    assert b == 1 and d == D_MODEL and w_in.shape[0] == 1
    bf = jnp.bfloat16
    w = w_in[0]
    wa = jnp.concatenate(
        [w[:, 0:1536], w[:, 2048:2752], jnp.zeros((d, 64), w.dtype)], axis=1).astype(bf)
    wg = jnp.concatenate([w[:, 1536:2048], w[:, 2752:5312]], axis=1).astype(bf)
    wuq = w_uq[0].reshape(MLA_Q_RANK, MLA_HEADS, MLA_NOPE + MLA_ROPE)
    wuq = jnp.concatenate(
        [wuq, jnp.zeros((MLA_Q_RANK, MLA_HEADS, 64), wuq.dtype)], axis=2)
    wuq = wuq.reshape(MLA_Q_RANK, MLA_HEADS * 256).astype(bf)
    wukv = w_ukv[0].reshape(MLA_KV_RANK, MLA_HEADS, MLA_NOPE + MLA_V)
    wuk = wukv[:, :, :MLA_NOPE].reshape(MLA_KV_RANK, MLA_HEADS * MLA_NOPE).astype(bf)
    wuv = wukv[:, :, MLA_NOPE:].reshape(MLA_KV_RANK, MLA_HEADS * MLA_V).astype(bf)

    inv_freq = ROPE_THETA ** (-jnp.arange(0, 64, 2, dtype=jnp.float32) / 64)
    invf = jnp.tile(inv_freq, 4).reshape(1, LANES)

    x2 = x.reshape(s, d)
    pos2 = positions.reshape(s, 1)
    qd, kd, vtd, qm, km, vtm = _input_stage(
        x2, pos2, invf, norm_in, wa, mla_q_norm, wuq, mla_kv_norm, wuk, wuv)

    lambdas = (diff_lambda_q1, diff_lambda_k1, diff_lambda_q2, diff_lambda_k2)
    od = _attention(qd, kd, vtd, lambdas, heads=1, n_maps=2, dk=LANES, name="diff_attention")
    om = _attention(qm, km, vtm, (), heads=2, n_maps=1, dk=2 * LANES, name="mla_attention")

    out = _output_stage(x2, od, om, norm_in, wg, diff_subln, w_proj_diff[0].astype(bf),
                        w_proj_mla[0].astype(bf), w_out[0].astype(bf),
                        norm_final.reshape(1, d))
    return out.reshape(b, s, d)
```

```python
import functools
import math

import jax
import jax.numpy as jnp
from jax import lax
from jax.experimental import pallas as pl
from jax.experimental.pallas import tpu as pltpu

D_MODEL = 1024
ROPE_THETA = 10000.0
NORM_EPS = 1e-6

DIFF_HEADS = 4
DIFF_HEAD_DIM = 64
DIFF_V_DIM = 128
MLA_HEADS = 4
MLA_NOPE = 128
MLA_ROPE = 64
MLA_V = 128
MLA_Q_RANK = 384
MLA_KV_RANK = 256

LANES = 128
LOG2E = math.log2(math.e)
MASK_VALUE = -1e30

ROW_BLOCK = 1024
KEY_ROWS = 512
CHUNK_COLS = 256
SUM_ROWS = 16
VT_ROWS = MLA_V + SUM_ROWS
VMEM_LIMIT_BYTES = 56 * 1024 * 1024

_W_QK = (1024, 0)
_W_LAT = (1024, 2)
_W_DGATE = (512, 3)
_L_CKV, _L_KR, _L_END = MLA_Q_RANK, MLA_Q_RANK + MLA_KV_RANK, MLA_Q_RANK + MLA_KV_RANK + LANES


def _rms(x, gain):
    return x * lax.rsqrt(jnp.mean(x * x, axis=-1, keepdims=True) + NORM_EPS) * gain


def _rope_tile(t, cos, sin_signed, first_half):
    partner = jnp.where(first_half, pltpu.roll(t, 96, axis=1), pltpu.roll(t, 32, axis=1))
    return t * cos + partner * sin_signed


def _rope_tables(pos, invf, lane_group):
    rows = pos.shape[0]
    quarter = rows // 4
    packed = pos[3 * quarter:, :]
    for j in (2, 1, 0):
        packed = jnp.where(lane_group == j, pos[j * quarter:(j + 1) * quarter, :], packed)
    ang = packed * invf

    def spread(tab):
        rolled = [tab] + [pltpu.roll(tab, 32 * k, axis=1) for k in (1, 2, 3)]
        parts = []
        for j in range(4):
            part = rolled[(3 - j) % 4]
            for g in (2, 1, 0):
                part = jnp.where(lane_group == g, rolled[(g - j) % 4], part)
            parts.append(part)
        return jnp.concatenate(parts, axis=0)

    return spread(jnp.cos(ang)), spread(jnp.sin(ang))


def _input_kernel(x_ref, pos_ref, invf_ref, gin_ref, wqk_ref, wvt_ref, wlat_ref, gq_ref, wuq_ref,
                  gkv_ref, wuk_ref, wuvt_ref,
                  qd_ref, kd_ref, vtd_ref, qm_ref, km_ref, vtm_ref,
                  *, diff_scale, mla_scale):
    x = x_ref[...]
    h = _rms(x, gin_ref[...]).astype(jnp.bfloat16)
    lat = jnp.dot(h, wlat_ref[:, :_L_END], preferred_element_type=jnp.float32)

    lane = lax.broadcasted_iota(jnp.int32, (1, LANES), 1)
    first_half = (lane & 63) < 32
    cos, sin = _rope_tables(pos_ref[...].astype(jnp.float32), invf_ref[...], lane >> 5)
    sin_signed = sin * jnp.where(first_half, -1.0, 1.0)
    rope = functools.partial(_rope_tile, cos=cos, sin_signed=sin_signed, first_half=first_half)
    ones = jnp.ones((SUM_ROWS, x.shape[0]), jnp.bfloat16)

    nt_dims = (((1,), (1,)), ((), ()))
    proj = jnp.dot(h, wqk_ref[...], preferred_element_type=jnp.float32)
    vt = lax.dot_general(wvt_ref[...], h, nt_dims, preferred_element_type=jnp.float32)
    for hd in range(DIFF_HEADS):
        c = hd * LANES
        q = rope(proj[:, c:c + LANES]) * diff_scale
        qd_ref[:, c:c + LANES] = q.astype(jnp.bfloat16)
        k = rope(proj[:, 512 + c:512 + c + LANES])
        kd_ref[:, c:c + LANES] = k.astype(jnp.bfloat16)
        vtd_ref[hd, :DIFF_V_DIM, :] = vt[c:c + DIFF_V_DIM, :].astype(jnp.bfloat16)
        vtd_ref[hd, DIFF_V_DIM:, :] = ones

    cq = _rms(lat[:, :_L_CKV], gq_ref[...]).astype(jnp.bfloat16)
    qm = jnp.dot(cq, wuq_ref[...], preferred_element_type=jnp.float32)
    ckv = _rms(lat[:, _L_CKV:_L_KR], gkv_ref[...]).astype(jnp.bfloat16)
    kn = jnp.dot(ckv, wuk_ref[...], preferred_element_type=jnp.float32)
    mvt = lax.dot_general(wuvt_ref[...], ckv, nt_dims, preferred_element_type=jnp.float32)
    kr = rope(jnp.where(lane < MLA_ROPE, lat[:, _L_KR:_L_END], 0.0)).astype(jnp.bfloat16)

    for hd in range(MLA_HEADS):
        c = hd * 2 * LANES
        qm_ref[:, c:c + LANES] = (qm[:, c:c + LANES] * mla_scale).astype(jnp.bfloat16)
        qr = rope(qm[:, c + LANES:c + 2 * LANES]) * mla_scale
        qm_ref[:, c + LANES:c + 2 * LANES] = qr.astype(jnp.bfloat16)
        km_ref[:, c:c + LANES] = kn[:, hd * LANES:(hd + 1) * LANES].astype(jnp.bfloat16)
        km_ref[:, c + LANES:c + 2 * LANES] = kr
        vtm_ref[hd, :MLA_V, :] = mvt[hd * MLA_V:(hd + 1) * MLA_V, :].astype(jnp.bfloat16)
        vtm_ref[hd, MLA_V:, :] = ones


def _input_stage(x2, pos2, invf, gin, w, wvt, gq, wuq, gkv, wuk, wuvt):
    s = x2.shape[0]
    tm = ROW_BLOCK
    full = lambda shape: pl.BlockSpec(shape, lambda i: (0,) * len(shape))
    rows = lambda n: pl.BlockSpec((tm, n), lambda i: (i, 0))
    vt_spec = pl.BlockSpec((DIFF_HEADS, VT_ROWS, tm), lambda i: (0, 0, i))
    wcols = lambda blk: pl.BlockSpec((D_MODEL, blk[0]), lambda i: (0, blk[1]))
    kern = functools.partial(
        _input_kernel,
        diff_scale=DIFF_HEAD_DIM ** -0.5 * LOG2E,
        mla_scale=(MLA_NOPE + MLA_ROPE) ** -0.5 * LOG2E)
    bf = jnp.bfloat16
    return pl.pallas_call(
        kern,
        grid=(s // tm,),
        in_specs=[rows(D_MODEL), rows(1), full(invf.shape), full(gin.shape), wcols(_W_QK),
                  full(wvt.shape), wcols(_W_LAT), full(gq.shape), full(wuq.shape),
                  full(gkv.shape), full(wuk.shape), full(wuvt.shape)],
        out_specs=[rows(512), rows(512), vt_spec, rows(1024), rows(1024), vt_spec],
        out_shape=[jax.ShapeDtypeStruct((s, 512), bf), jax.ShapeDtypeStruct((s, 512), bf),
                   jax.ShapeDtypeStruct((DIFF_HEADS, VT_ROWS, s), bf),
                   jax.ShapeDtypeStruct((s, 1024), bf), jax.ShapeDtypeStruct((s, 1024), bf),
                   jax.ShapeDtypeStruct((MLA_HEADS, VT_ROWS, s), bf)],
        compiler_params=pltpu.CompilerParams(
            dimension_semantics=("parallel",), vmem_limit_bytes=VMEM_LIMIT_BYTES),
        name="input_stage",
    )(x2, pos2, invf, gin, w, wvt, w, gq, wuq, gkv, wuk, wuvt)


_M, _BMAX, _ALPHA = 0, (1, 2), (3, 4)


class _AttnRefs:
    def __init__(self, k_ref, vt_ref, qc_ref, stat_ref, acc_ref, s_refs, p_refs):
        self.k, self.vt, self.qc, self.stat, self.acc = k_ref, vt_ref, qc_ref, stat_ref, acc_ref
        self.s, self.p = s_refs, p_refs

    def row(self, i, cols):
        return self.stat.at[i:i + 1, cols]


def _score_stage(r, start, slot, diag, cols, head, tq):
    dk = r.qc.shape[0]
    tk = r.s[slot].shape[0]
    ncols = cols.stop - cols.start
    j0 = cols.start % tq
    if diag is not None and diag * tk >= j0 + ncols:
        r.s[slot][:, cols] = jnp.full((tk, ncols), MASK_VALUE, jnp.float32)
        r.row(_BMAX[slot], cols)[...] = jnp.full((1, ncols), MASK_VALUE, jnp.float32)
        return
    kb = r.k[pl.ds(start, tk), head * dk:(head + 1) * dk]
    st = jnp.dot(kb, r.qc[:, cols], preferred_element_type=jnp.float32)
    if diag is not None and diag * tk + tk - 1 > j0:
        i = lax.broadcasted_iota(jnp.int32, (tk, ncols), 0) + diag * tk
        j = lax.broadcasted_iota(jnp.int32, (tk, ncols), 1) + j0
        st = jnp.where(i <= j, st, MASK_VALUE)
    r.s[slot][:, cols] = st
    r.row(_BMAX[slot], cols)[...] = jnp.max(st, axis=0, keepdims=True)


def _softmax_stage(r, slot, cols, first=False):
    m_old = (jnp.full((1, cols.stop - cols.start), MASK_VALUE, jnp.float32) if first
             else r.row(_M, cols)[...])
    m_new = jnp.maximum(m_old, r.row(_BMAX[slot], cols)[...])
    r.row(_ALPHA[slot], cols)[...] = jnp.exp2(m_old - m_new)
    r.row(_M, cols)[...] = m_new
    r.p[slot][:, cols] = jnp.exp2(r.s[slot][:, cols] - m_new).astype(jnp.bfloat16)


def _value_stage(r, start, slot, cols, head):
    vtb = r.vt[head, :, pl.ds(start, r.p[slot].shape[0])]
    pv = jnp.dot(vtb, r.p[slot][:, cols], preferred_element_type=jnp.float32)
    r.acc[:, cols] = r.row(_ALPHA[slot], cols)[...] * r.acc[:, cols] + pv


def _attn_kernel(*refs, heads, n_maps, tq, tk, nq):
    n_lam = 1 if n_maps == 2 else 0
    q_ref, k_ref, vt_ref = refs[:3]
    lam_ref = refs[3] if n_lam else None
    o_ref = refs[3 + n_lam]
    qc_ref, stat_ref, acc_ref, s0, s1, p0, p1 = refs[4 + n_lam:]
    r = _AttnRefs(k_ref, vt_ref, qc_ref, stat_ref, acc_ref, (s0, s1), (p0, p1))

    assert tq == 2 * tk
    dk = qc_ref.shape[0]
    ncol = heads * n_maps * tq
    qi = pl.program_id(1)
    prev = jnp.maximum(qi - 1, 0)

    def key_start(b, t):
        blk = jnp.where(t < 2, b * 2 + t, t - 2)
        return pl.multiple_of(blk * tk, tk)

    dv = o_ref.shape[1] // heads

    def normalized(c0):
        cols = slice(c0, c0 + CHUNK_COLS)
        return acc_ref[:dv, cols] * (1.0 / acc_ref[dv:dv + 1, cols])

    def finish_chunk(c0):
        head, g0 = divmod(c0, n_maps * tq)
        if n_maps == 2:
            if g0 < tq:
                return
            lam_init = 0.8 - 0.6 * math.exp(-0.3 * 0)
            lam = (jnp.exp(jnp.sum(lam_ref[0:1, :] * lam_ref[1:2, :], axis=-1, keepdims=True))
                   - jnp.exp(jnp.sum(lam_ref[2:3, :] * lam_ref[3:4, :], axis=-1, keepdims=True))
                   + lam_init)
            oh = normalized(c0 - tq) - lam * normalized(c0)
        else:
            oh = normalized(c0)
        row0 = g0 % tq
        o_ref[row0:row0 + CHUNK_COLS, head * dv:(head + 1) * dv] = oh.T

    def stages(score=None, softmax=None, values=(), first=False, finish=False):
        for c0 in range(0, ncol, CHUNK_COLS):
            cols = slice(c0, c0 + CHUNK_COLS)
            head = c0 // (n_maps * tq)
            if softmax is not None:
                _softmax_stage(r, softmax, cols, first)
            if score is not None:
                _score_stage(r, score[0], score[1], score[2], cols, head, tq)
            for start, slot in values:
                _value_stage(r, start, slot, cols, head)
            if finish:
                finish_chunk(c0)

    prev_a = key_start(prev, 2 * prev)
    prev_b = key_start(prev, 2 * prev + 1)

    @pl.when(qi == 0)
    def _():
        for ref in (s1, p0, p1, stat_ref, acc_ref):
            ref[...] = jnp.zeros(ref.shape, ref.dtype)

    @pl.when(qi < nq)
    def _():
        for h in range(heads):
            qt = q_ref[:, h * dk:(h + 1) * dk].astype(jnp.float32).T
            if n_maps == 2:
                feat = lax.broadcasted_iota(jnp.int32, qt.shape, 0)
                zero = jnp.zeros_like(qt)
                qc_ref[:, 2 * h * tq:(2 * h + 1) * tq] = jnp.where(
                    feat < DIFF_HEAD_DIM, qt, zero).astype(jnp.bfloat16)
                qc_ref[:, (2 * h + 1) * tq:(2 * h + 2) * tq] = jnp.where(
                    feat >= DIFF_HEAD_DIM, qt, zero).astype(jnp.bfloat16)
            else:
                qc_ref[:, h * tq:(h + 1) * tq] = qt.astype(jnp.bfloat16)

        stages(score=(key_start(qi, 0), 0, 0), softmax=1, values=[(prev_a, 0)])
        stages(score=(key_start(qi, 1), 1, 1), softmax=0, values=[(prev_b, 1)], first=True,
               finish=True)

        def pair(u, carry):
            t = 2 * u
            stages(score=(key_start(qi, t), 0, None), softmax=1,
                   values=[(key_start(qi, t - 2), 0)])
            stages(score=(key_start(qi, t + 1), 1, None), softmax=0,
                   values=[(key_start(qi, t - 1), 1)])
            return carry

        lax.fori_loop(1, qi + 1, pair, 0)

    @pl.when(qi == nq)
    def _():
        stages(softmax=1, values=[(prev_a, 0), (prev_b, 1)], finish=True)


def _attention(q, k, vt, lambdas, *, heads, n_maps, dk, name):
    s = q.shape[0]
    n_heads, vt_rows, _ = vt.shape
    dv = vt_rows - SUM_ROWS
    tk, tq = KEY_ROWS, 2 * KEY_ROWS
    nq = s // tq
    ncol = heads * n_maps * tq
    f32, bf = jnp.float32, jnp.bfloat16
    kv_bytes = heads * s * (dk + vt_rows) * 2
    kv_mode = dict(pipeline_mode=pl.Buffered(1)) if 2 * kv_bytes > VMEM_LIMIT_BYTES // 2 else {}
    in_specs = [pl.BlockSpec((tq, heads * dk), lambda h, i: (jnp.minimum(i, nq - 1), h)),
                pl.BlockSpec((s, heads * dk), lambda h, i: (0, h), **kv_mode),
                pl.BlockSpec((heads, vt_rows, s), lambda h, i: (h, 0, 0), **kv_mode)]
    in_specs += [pl.BlockSpec(lam.shape, lambda h, i: (0, 0)) for lam in lambdas]
    scratch = [pltpu.VMEM((dk, ncol), bf), pltpu.VMEM((8, ncol), f32),
               pltpu.VMEM((vt_rows, ncol), f32),
               pltpu.VMEM((tk, ncol), f32), pltpu.VMEM((tk, ncol), f32),
               pltpu.VMEM((tk, ncol), bf), pltpu.VMEM((tk, ncol), bf)]
    return pl.pallas_call(
        functools.partial(_attn_kernel, heads=heads, n_maps=n_maps, tq=tq, tk=tk, nq=nq),
        grid=(n_heads // heads, nq + 1),
        in_specs=in_specs,
        out_specs=pl.BlockSpec((tq, heads * dv), lambda h, i: (jnp.maximum(i - 1, 0), h)),
        out_shape=jax.ShapeDtypeStruct((s, n_heads * dv), jnp.float32),
        scratch_shapes=scratch,
        compiler_params=pltpu.CompilerParams(
            dimension_semantics=("arbitrary", "arbitrary"), vmem_limit_bytes=VMEM_LIMIT_BYTES),
        name=name,
    )(q, k, vt, *lambdas)


def _output_kernel(x_ref, od_ref, om_ref, gin_ref, wdg_ref, wg_ref, subln_ref, wpd_ref, wpm_ref,
                   wout_ref, gfin_ref, out_ref, *, lam_init):
    x = x_ref[...]
    h = _rms(x, gin_ref[...]).astype(jnp.bfloat16)
    dgate = jnp.dot(h, wdg_ref[...], preferred_element_type=jnp.float32)
    gates = jnp.dot(h, wg_ref[...], preferred_element_type=jnp.float32)
    mgate, g_diff, g_mla = gates[:, 0:512], gates[:, 512:1536], gates[:, 1536:2560]

    od = od_ref[...]
    subln = subln_ref[...]
    od = jnp.concatenate(
        [_rms(od[:, hd * LANES:(hd + 1) * LANES], subln) for hd in range(DIFF_HEADS)], axis=1)
    od = od * (1.0 - lam_init) * jax.nn.silu(dgate)
    om = om_ref[...] * jax.nn.silu(mgate)

    pd = jnp.dot(od.astype(jnp.bfloat16), wpd_ref[...], preferred_element_type=jnp.float32)
    pm = jnp.dot(om.astype(jnp.bfloat16), wpm_ref[...], preferred_element_type=jnp.float32)
    merged = jax.nn.sigmoid(g_diff) * pd + jax.nn.sigmoid(g_mla) * pm
    y = x + jnp.dot(merged.astype(jnp.bfloat16), wout_ref[...],
                    preferred_element_type=jnp.float32)
    out_ref[...] = _rms(y, gfin_ref[...])


def _output_stage(x2, od, om, gin, w, wg, subln, wpd, wpm, wout, gfin):
    s = x2.shape[0]
    tm = ROW_BLOCK
    full = lambda shape: pl.BlockSpec(shape, lambda i: (0,) * len(shape))
    rows = lambda n: pl.BlockSpec((tm, n), lambda i: (i, 0))
    lam_init = 0.8 - 0.6 * math.exp(-0.3 * 0)
    return pl.pallas_call(
        functools.partial(_output_kernel, lam_init=lam_init),
        grid=(s // tm,),
        in_specs=[rows(D_MODEL), rows(512), rows(512), full(gin.shape),
                  pl.BlockSpec((D_MODEL, _W_DGATE[0]), lambda i: (0, _W_DGATE[1])),
                  full(wg.shape), full(subln.shape), full(wpd.shape), full(wpm.shape),
                  full(wout.shape), full(gfin.shape)],
        out_specs=rows(D_MODEL),
        out_shape=jax.ShapeDtypeStruct((s, D_MODEL), jnp.float32),
        compiler_params=pltpu.CompilerParams(
            dimension_semantics=("parallel",), vmem_limit_bytes=VMEM_LIMIT_BYTES),
        name="output_stage",
    )(x2, od, om, gin, w, wg, subln, wpd, wpm, wout, gfin)


def kernel(x, positions, norm_in, w_in, diff_lambda_q1, diff_lambda_k1, diff_lambda_q2,
           diff_lambda_k2, diff_subln, mla_q_norm, w_uq, mla_kv_norm, w_ukv, w_proj_diff,
           w_proj_mla, w_out, norm_final):
    b, s, d = x.shape
    assert b == 1 and d == D_MODEL and w_in.shape[0] == 1
    bf = jnp.bfloat16
    w = w_in[0].astype(bf)
    wg = w[:, 2752:5312]
    wuq = w_uq[0].reshape(MLA_Q_RANK, MLA_HEADS, MLA_NOPE + MLA_ROPE)
    wuq = jnp.concatenate(
        [wuq, jnp.zeros((MLA_Q_RANK, MLA_HEADS, 64), wuq.dtype)], axis=2)
    wuq = wuq.reshape(MLA_Q_RANK, MLA_HEADS * 256).astype(bf)
    wukv = w_ukv[0].reshape(MLA_KV_RANK, MLA_HEADS, MLA_NOPE + MLA_V)
    wuk = wukv[:, :, :MLA_NOPE].reshape(MLA_KV_RANK, MLA_HEADS * MLA_NOPE).astype(bf)
    wuvt = wukv[:, :, MLA_NOPE:].reshape(MLA_KV_RANK, MLA_HEADS * MLA_V).T.astype(bf)
    wvt = w[:, 1024:1536].T

    inv_freq = ROPE_THETA ** (-jnp.arange(0, 64, 2, dtype=jnp.float32) / 64)
    invf = jnp.tile(inv_freq, 4).reshape(1, LANES)

    x2 = x.reshape(s, d)
    pos2 = positions.reshape(s, 1)
    qd, kd, vtd, qm, km, vtm = _input_stage(
        x2, pos2, invf, norm_in, w, wvt, mla_q_norm, wuq, mla_kv_norm, wuk, wuvt)

    lam_rows = jnp.concatenate(
        [diff_lambda_q1, diff_lambda_k1, diff_lambda_q2, diff_lambda_k2], axis=0)
    lambdas = (jnp.pad(lam_rows, ((0, 28), (0, LANES - DIFF_HEAD_DIM))),)
    od = _attention(qd, kd, vtd, lambdas, heads=1, n_maps=2, dk=LANES, name="diff_attention")
    om = _attention(qm, km, vtm, (), heads=2, n_maps=1, dk=2 * LANES, name="mla_attention")

    out = _output_stage(x2, od, om, norm_in, w, wg, diff_subln, w_proj_diff[0].astype(bf),
                        w_proj_mla[0].astype(bf), w_out[0].astype(bf),
                        norm_final.reshape(1, d))
    return out.reshape(b, s, d)
```

```python
import functools
import math

import jax
import jax.numpy as jnp
from jax import lax
from jax.experimental import pallas as pl
from jax.experimental.pallas import tpu as pltpu

D_MODEL = 1024
ROPE_THETA = 10000.0
NORM_EPS = 1e-6

DIFF_HEADS = 4
DIFF_HEAD_DIM = 64
DIFF_V_DIM = 128
MLA_HEADS = 4
MLA_NOPE = 128
MLA_ROPE = 64
MLA_V = 128
MLA_Q_RANK = 384
MLA_KV_RANK = 256

LANES = 128
LOG2E = math.log2(math.e)
MASK_VALUE = -1e30

ROW_BLOCK = 1024
KEY_ROWS = 512
CHUNK_COLS = 256
SUM_ROWS = 16
VT_ROWS = MLA_V + SUM_ROWS
VMEM_LIMIT_BYTES = 56 * 1024 * 1024

_W_QK = (1024, 0)
_W_LAT = (1024, 2)
_W_DGATE = (512, 3)
_L_CKV, _L_KR, _L_END = MLA_Q_RANK, MLA_Q_RANK + MLA_KV_RANK, MLA_Q_RANK + MLA_KV_RANK + LANES


def _rms(x, gain):
    return x * lax.rsqrt(jnp.mean(x * x, axis=-1, keepdims=True) + NORM_EPS) * gain


def _rope_tile(t, cos, sin_signed, first_half):
    partner = jnp.where(first_half, pltpu.roll(t, 96, axis=1), pltpu.roll(t, 32, axis=1))
    return t * cos + partner * sin_signed


def _rope_tables(pos, invf, lane_group):
    rows = pos.shape[0]
    quarter = rows // 4
    packed = pos[3 * quarter:, :]
    for j in (2, 1, 0):
        packed = jnp.where(lane_group == j, pos[j * quarter:(j + 1) * quarter, :], packed)
    ang = packed * invf

    def spread(tab):
        rolled = [tab] + [pltpu.roll(tab, 32 * k, axis=1) for k in (1, 2, 3)]
        parts = []
        for j in range(4):
            part = rolled[(3 - j) % 4]
            for g in (2, 1, 0):
                part = jnp.where(lane_group == g, rolled[(g - j) % 4], part)
            parts.append(part)
        return jnp.concatenate(parts, axis=0)

    return spread(jnp.cos(ang)), spread(jnp.sin(ang))


def _input_kernel(x_ref, pos_ref, invf_ref, gin_ref, wqk_ref, wvt_ref, wlat_ref, gq_ref, wuq_ref,
                  gkv_ref, wuk_ref, wuvt_ref,
                  qd_ref, kd_ref, vtd_ref, qm_ref, km_ref, vtm_ref,
                  *, diff_scale, mla_scale):
    x = x_ref[...]
    h = _rms(x, gin_ref[...]).astype(jnp.bfloat16)
    lat = jnp.dot(h, wlat_ref[:, :_L_END], preferred_element_type=jnp.float32)

    lane = lax.broadcasted_iota(jnp.int32, (1, LANES), 1)
    first_half = (lane & 63) < 32
    cos, sin = _rope_tables(pos_ref[...].astype(jnp.float32), invf_ref[...], lane >> 5)
    sin_signed = sin * jnp.where(first_half, -1.0, 1.0)
    rope = functools.partial(_rope_tile, cos=cos, sin_signed=sin_signed, first_half=first_half)
    ones = jnp.ones((SUM_ROWS, x.shape[0]), jnp.bfloat16)

    nt_dims = (((1,), (1,)), ((), ()))
    proj = jnp.dot(h, wqk_ref[...], preferred_element_type=jnp.float32)
    vt = lax.dot_general(wvt_ref[...], h, nt_dims, preferred_element_type=jnp.float32)
    for hd in range(DIFF_HEADS):
        c = hd * LANES
        q = rope(proj[:, c:c + LANES]) * diff_scale
        qd_ref[:, c:c + LANES] = q.astype(jnp.bfloat16)
        k = rope(proj[:, 512 + c:512 + c + LANES])
        kd_ref[:, c:c + LANES] = k.astype(jnp.bfloat16)
        vtd_ref[hd, :DIFF_V_DIM, :] = vt[c:c + DIFF_V_DIM, :].astype(jnp.bfloat16)
        vtd_ref[hd, DIFF_V_DIM:, :] = ones

    cq = _rms(lat[:, :_L_CKV], gq_ref[...]).astype(jnp.bfloat16)
    qm = jnp.dot(cq, wuq_ref[...], preferred_element_type=jnp.float32)
    ckv = _rms(lat[:, _L_CKV:_L_KR], gkv_ref[...]).astype(jnp.bfloat16)
    kn = jnp.dot(ckv, wuk_ref[...], preferred_element_type=jnp.float32)
    mvt = lax.dot_general(wuvt_ref[...], ckv, nt_dims, preferred_element_type=jnp.float32)
    kr = rope(jnp.where(lane < MLA_ROPE, lat[:, _L_KR:_L_END], 0.0)).astype(jnp.bfloat16)

    for hd in range(MLA_HEADS):
        c = hd * 2 * LANES
        qm_ref[:, c:c + LANES] = (qm[:, c:c + LANES] * mla_scale).astype(jnp.bfloat16)
        qr = rope(qm[:, c + LANES:c + 2 * LANES]) * mla_scale
        qm_ref[:, c + LANES:c + 2 * LANES] = qr.astype(jnp.bfloat16)
        km_ref[:, c:c + LANES] = kn[:, hd * LANES:(hd + 1) * LANES].astype(jnp.bfloat16)
        km_ref[:, c + LANES:c + 2 * LANES] = kr
        vtm_ref[hd, :MLA_V, :] = mvt[hd * MLA_V:(hd + 1) * MLA_V, :].astype(jnp.bfloat16)
        vtm_ref[hd, MLA_V:, :] = ones


def _input_stage(x2, pos2, invf, gin, w, wvt, gq, wuq, gkv, wuk, wuvt):
    s = x2.shape[0]
    tm = ROW_BLOCK
    full = lambda shape: pl.BlockSpec(shape, lambda i: (0,) * len(shape))
    rows = lambda n: pl.BlockSpec((tm, n), lambda i: (i, 0))
    vt_spec = pl.BlockSpec((DIFF_HEADS, VT_ROWS, tm), lambda i: (0, 0, i))
    wcols = lambda blk: pl.BlockSpec((D_MODEL, blk[0]), lambda i: (0, blk[1]))
    kern = functools.partial(
        _input_kernel,
        diff_scale=DIFF_HEAD_DIM ** -0.5 * LOG2E,
        mla_scale=(MLA_NOPE + MLA_ROPE) ** -0.5 * LOG2E)
    bf = jnp.bfloat16
    return pl.pallas_call(
        kern,
        grid=(s // tm,),
        in_specs=[rows(D_MODEL), rows(1), full(invf.shape), full(gin.shape), wcols(_W_QK),
                  full(wvt.shape), wcols(_W_LAT), full(gq.shape), full(wuq.shape),
                  full(gkv.shape), full(wuk.shape), full(wuvt.shape)],
        out_specs=[rows(512), rows(512), vt_spec, rows(1024), rows(1024), vt_spec],
        out_shape=[jax.ShapeDtypeStruct((s, 512), bf), jax.ShapeDtypeStruct((s, 512), bf),
                   jax.ShapeDtypeStruct((DIFF_HEADS, VT_ROWS, s), bf),
                   jax.ShapeDtypeStruct((s, 1024), bf), jax.ShapeDtypeStruct((s, 1024), bf),
                   jax.ShapeDtypeStruct((MLA_HEADS, VT_ROWS, s), bf)],
        compiler_params=pltpu.CompilerParams(
            dimension_semantics=("parallel",), vmem_limit_bytes=VMEM_LIMIT_BYTES),
        name="input_stage",
    )(x2, pos2, invf, gin, w, wvt, w, gq, wuq, gkv, wuk, wuvt)


_M, _BMAX, _ALPHA = 0, (1, 2), (3, 4)


class _AttnRefs:
    def __init__(self, k_ref, vt_ref, qc_ref, stat_ref, acc_ref, s_refs, p_refs):
        self.k, self.vt, self.qc, self.stat, self.acc = k_ref, vt_ref, qc_ref, stat_ref, acc_ref
        self.s, self.p = s_refs, p_refs

    def row(self, i, cols):
        return self.stat.at[i:i + 1, cols]


def _score_stage(r, start, slot, diag, cols, head, tq):
    dk = r.qc.shape[0]
    tk = r.s[slot].shape[0]
    ncols = cols.stop - cols.start
    j0 = cols.start % tq
    if diag is not None and diag * tk >= j0 + ncols:
        r.s[slot][:, cols] = jnp.full((tk, ncols), MASK_VALUE, jnp.float32)
        r.row(_BMAX[slot], cols)[...] = jnp.full((1, ncols), MASK_VALUE, jnp.float32)
        return
    kb = r.k[pl.ds(start, tk), head * dk:(head + 1) * dk]
    st = jnp.dot(kb, r.qc[:, cols], preferred_element_type=jnp.float32)
    if diag is not None and diag * tk + tk - 1 > j0:
        i = lax.broadcasted_iota(jnp.int32, (tk, ncols), 0) + diag * tk
        j = lax.broadcasted_iota(jnp.int32, (tk, ncols), 1) + j0
        st = jnp.where(i <= j, st, MASK_VALUE)
    r.s[slot][:, cols] = st
    r.row(_BMAX[slot], cols)[...] = jnp.max(st, axis=0, keepdims=True)


def _softmax_stage(r, slot, cols, first=False):
    m_old = (jnp.full((1, cols.stop - cols.start), MASK_VALUE, jnp.float32) if first
             else r.row(_M, cols)[...])
    m_new = jnp.maximum(m_old, r.row(_BMAX[slot], cols)[...])
    r.row(_ALPHA[slot], cols)[...] = jnp.exp2(m_old - m_new)
    r.row(_M, cols)[...] = m_new
    r.p[slot][:, cols] = jnp.exp2(r.s[slot][:, cols] - m_new).astype(jnp.bfloat16)


def _value_stage(r, start, slot, cols, head):
    vtb = r.vt[head, :, pl.ds(start, r.p[slot].shape[0])]
    pv = jnp.dot(vtb, r.p[slot][:, cols], preferred_element_type=jnp.float32)
    r.acc[:, cols] = r.row(_ALPHA[slot], cols)[...] * r.acc[:, cols] + pv


def _attn_kernel(*refs, heads, n_maps, tq, tk, nq):
    n_lam = 1 if n_maps == 2 else 0
    q_ref, k_ref, vt_ref = refs[:3]
    lam_ref = refs[3] if n_lam else None
    o_ref = refs[3 + n_lam]
    qc_ref, stat_ref, acc_ref, s0, s1, p0, p1 = refs[4 + n_lam:]
    r = _AttnRefs(k_ref, vt_ref, qc_ref, stat_ref, acc_ref, (s0, s1), (p0, p1))

    assert tq == 2 * tk
    dk = qc_ref.shape[0]
    ncol = heads * n_maps * tq
    qi = pl.program_id(1)
    prev = jnp.maximum(qi - 1, 0)

    def key_start(b, t):
        blk = jnp.where(t < 2, b * 2 + t, t - 2)
        return pl.multiple_of(blk * tk, tk)

    dv = o_ref.shape[1] // heads

    def normalized(c0):
        cols = slice(c0, c0 + CHUNK_COLS)
        return acc_ref[:dv, cols] * (1.0 / acc_ref[dv:dv + 1, cols])

    def finish_chunk(c0):
        head, g0 = divmod(c0, n_maps * tq)
        if n_maps == 2:
            if g0 < tq:
                return
            lam_init = 0.8 - 0.6 * math.exp(-0.3 * 0)
            lam = (jnp.exp(jnp.sum(lam_ref[0:1, :] * lam_ref[1:2, :], axis=-1, keepdims=True))
                   - jnp.exp(jnp.sum(lam_ref[2:3, :] * lam_ref[3:4, :], axis=-1, keepdims=True))
                   + lam_init)
            oh = normalized(c0 - tq) - lam * normalized(c0)
        else:
            oh = normalized(c0)
        row0 = g0 % tq
        o_ref[row0:row0 + CHUNK_COLS, head * dv:(head + 1) * dv] = oh.T

    def pair_steps(score_starts, value_starts, new_block):
        drain = score_starts is None
        for c0 in range(0, ncol, CHUNK_COLS):
            cols = slice(c0, c0 + CHUNK_COLS)
            head = c0 // (n_maps * tq)
            for step in range(2):
                if not (drain and step == 1):
                    _softmax_stage(r, 1 - step, cols, first=new_block and step == 1)
                if not drain:
                    _score_stage(r, score_starts[step], step, step if new_block else None,
                                 cols, head, tq)
                _value_stage(r, value_starts[step], step, cols, head)
            if new_block or drain:
                finish_chunk(c0)

    prev_a = key_start(prev, 2 * prev)
    prev_b = key_start(prev, 2 * prev + 1)

    @pl.when(qi == 0)
    def _():
        for ref in (s1, p0, p1, stat_ref, acc_ref):
            ref[...] = jnp.zeros(ref.shape, ref.dtype)

    @pl.when(qi < nq)
    def _():
        for h in range(heads):
            qt = q_ref[:, h * dk:(h + 1) * dk].astype(jnp.float32).T
            if n_maps == 2:
                feat = lax.broadcasted_iota(jnp.int32, qt.shape, 0)
                zero = jnp.zeros_like(qt)
                qc_ref[:, 2 * h * tq:(2 * h + 1) * tq] = jnp.where(
                    feat < DIFF_HEAD_DIM, qt, zero).astype(jnp.bfloat16)
                qc_ref[:, (2 * h + 1) * tq:(2 * h + 2) * tq] = jnp.where(
                    feat >= DIFF_HEAD_DIM, qt, zero).astype(jnp.bfloat16)
            else:
                qc_ref[:, h * tq:(h + 1) * tq] = qt.astype(jnp.bfloat16)

        pair_steps((key_start(qi, 0), key_start(qi, 1)), (prev_a, prev_b), new_block=True)

        def pair(u, carry):
            t = 2 * u
            pair_steps((key_start(qi, t), key_start(qi, t + 1)),
                       (key_start(qi, t - 2), key_start(qi, t - 1)), new_block=False)
            return carry

        lax.fori_loop(1, qi + 1, pair, 0)

    @pl.when(qi == nq)
    def _():
        pair_steps(None, (prev_a, prev_b), new_block=False)


def _attention(q, k, vt, lambdas, *, heads, n_maps, dk, name):
    s = q.shape[0]
    n_heads, vt_rows, _ = vt.shape
    dv = vt_rows - SUM_ROWS
    tk, tq = KEY_ROWS, 2 * KEY_ROWS
    nq = s // tq
    ncol = heads * n_maps * tq
    f32, bf = jnp.float32, jnp.bfloat16
    kv_bytes = heads * s * (dk + vt_rows) * 2
    kv_mode = dict(pipeline_mode=pl.Buffered(1)) if 2 * kv_bytes > VMEM_LIMIT_BYTES // 2 else {}
    in_specs = [pl.BlockSpec((tq, heads * dk), lambda h, i: (jnp.minimum(i, nq - 1), h)),
                pl.BlockSpec((s, heads * dk), lambda h, i: (0, h), **kv_mode),
                pl.BlockSpec((heads, vt_rows, s), lambda h, i: (h, 0, 0), **kv_mode)]
    in_specs += [pl.BlockSpec(lam.shape, lambda h, i: (0, 0)) for lam in lambdas]
    scratch = [pltpu.VMEM((dk, ncol), bf), pltpu.VMEM((8, ncol), f32),
               pltpu.VMEM((vt_rows, ncol), f32),
               pltpu.VMEM((tk, ncol), f32), pltpu.VMEM((tk, ncol), f32),
               pltpu.VMEM((tk, ncol), bf), pltpu.VMEM((tk, ncol), bf)]
    return pl.pallas_call(
        functools.partial(_attn_kernel, heads=heads, n_maps=n_maps, tq=tq, tk=tk, nq=nq),
        grid=(n_heads // heads, nq + 1),
        in_specs=in_specs,
        out_specs=pl.BlockSpec((tq, heads * dv), lambda h, i: (jnp.maximum(i - 1, 0), h)),
        out_shape=jax.ShapeDtypeStruct((s, n_heads * dv), jnp.float32),
        scratch_shapes=scratch,
        compiler_params=pltpu.CompilerParams(
            dimension_semantics=("arbitrary", "arbitrary"), vmem_limit_bytes=VMEM_LIMIT_BYTES),
        name=name,
    )(q, k, vt, *lambdas)


def _output_kernel(x_ref, od_ref, om_ref, gin_ref, wdg_ref, wg_ref, subln_ref, wpd_ref, wpm_ref,
                   wout_ref, gfin_ref, out_ref, *, lam_init):
    x = x_ref[...]
    h = _rms(x, gin_ref[...]).astype(jnp.bfloat16)
    dgate = jnp.dot(h, wdg_ref[...], preferred_element_type=jnp.float32)
    gates = jnp.dot(h, wg_ref[...], preferred_element_type=jnp.float32)
    mgate, g_diff, g_mla = gates[:, 0:512], gates[:, 512:1536], gates[:, 1536:2560]

    od = od_ref[...]
    subln = subln_ref[...]
    od = jnp.concatenate(
        [_rms(od[:, hd * LANES:(hd + 1) * LANES], subln) for hd in range(DIFF_HEADS)], axis=1)
    od = od * (1.0 - lam_init) * jax.nn.silu(dgate)
    om = om_ref[...] * jax.nn.silu(mgate)

    pd = jnp.dot(od.astype(jnp.bfloat16), wpd_ref[...], preferred_element_type=jnp.float32)
    pm = jnp.dot(om.astype(jnp.bfloat16), wpm_ref[...], preferred_element_type=jnp.float32)
    merged = jax.nn.sigmoid(g_diff) * pd + jax.nn.sigmoid(g_mla) * pm
    y = x + jnp.dot(merged.astype(jnp.bfloat16), wout_ref[...],
                    preferred_element_type=jnp.float32)
    out_ref[...] = _rms(y, gfin_ref[...])


def _output_stage(x2, od, om, gin, w, wg, subln, wpd, wpm, wout, gfin):
    s = x2.shape[0]
    tm = ROW_BLOCK
    full = lambda shape: pl.BlockSpec(shape, lambda i: (0,) * len(shape))
    rows = lambda n: pl.BlockSpec((tm, n), lambda i: (i, 0))
    lam_init = 0.8 - 0.6 * math.exp(-0.3 * 0)
    return pl.pallas_call(
        functools.partial(_output_kernel, lam_init=lam_init),
        grid=(s // tm,),
        in_specs=[rows(D_MODEL), rows(512), rows(512), full(gin.shape),
                  pl.BlockSpec((D_MODEL, _W_DGATE[0]), lambda i: (0, _W_DGATE[1])),
                  full(wg.shape), full(subln.shape), full(wpd.shape), full(wpm.shape),
                  full(wout.shape), full(gfin.shape)],
        out_specs=rows(D_MODEL),
        out_shape=jax.ShapeDtypeStruct((s, D_MODEL), jnp.float32),
        compiler_params=pltpu.CompilerParams(
            dimension_semantics=("parallel",), vmem_limit_bytes=VMEM_LIMIT_BYTES),
        name="output_stage",
    )(x2, od, om, gin, w, wg, subln, wpd, wpm, wout, gfin)


def kernel(x, positions, norm_in, w_in, diff_lambda_q1, diff_lambda_k1, diff_lambda_q2,
           diff_lambda_k2, diff_subln, mla_q_norm, w_uq, mla_kv_norm, w_ukv, w_proj_diff,
           w_proj_mla, w_out, norm_final):
    b, s, d = x.shape
    assert b == 1 and d == D_MODEL and w_in.shape[0] == 1
    bf = jnp.bfloat16
    w = w_in[0].astype(bf)
    wg = w[:, 2752:5312]
    wuq = w_uq[0].reshape(MLA_Q_RANK, MLA_HEADS, MLA_NOPE + MLA_ROPE)
    wuq = jnp.concatenate(
        [wuq, jnp.zeros((MLA_Q_RANK, MLA_HEADS, 64), wuq.dtype)], axis=2)
    wuq = wuq.reshape(MLA_Q_RANK, MLA_HEADS * 256).astype(bf)
    wukv = w_ukv[0].reshape(MLA_KV_RANK, MLA_HEADS, MLA_NOPE + MLA_V)
    wuk = wukv[:, :, :MLA_NOPE].reshape(MLA_KV_RANK, MLA_HEADS * MLA_NOPE).astype(bf)
    wuvt = wukv[:, :, MLA_NOPE:].reshape(MLA_KV_RANK, MLA_HEADS * MLA_V).T.astype(bf)
    wvt = w[:, 1024:1536].T

    inv_freq = ROPE_THETA ** (-jnp.arange(0, 64, 2, dtype=jnp.float32) / 64)
    invf = jnp.tile(inv_freq, 4).reshape(1, LANES)

    x2 = x.reshape(s, d)
    pos2 = positions.reshape(s, 1)
    qd, kd, vtd, qm, km, vtm = _input_stage(
        x2, pos2, invf, norm_in, w, wvt, mla_q_norm, wuq, mla_kv_norm, wuk, wuvt)

    lam_rows = jnp.concatenate(
        [diff_lambda_q1, diff_lambda_k1, diff_lambda_q2, diff_lambda_k2], axis=0)
    lambdas = (jnp.pad(lam_rows, ((0, 28), (0, LANES - DIFF_HEAD_DIM))),)
    od = _attention(qd, kd, vtd, lambdas, heads=1, n_maps=2, dk=LANES, name="diff_attention")
    om = _attention(qm, km, vtm, (), heads=2, n_maps=1, dk=2 * LANES, name="mla_attention")

    out = _output_stage(x2, od, om, norm_in, w, wg, diff_subln, w_proj_diff[0].astype(bf),
                        w_proj_mla[0].astype(bf), w_out[0].astype(bf),
                        norm_final.reshape(1, d))
    return out.reshape(b, s, d)
```

```python
import functools
import math

import jax
import jax.numpy as jnp
from jax import lax
from jax.experimental import pallas as pl
from jax.experimental.pallas import tpu as pltpu

D_MODEL = 1024
ROPE_THETA = 10000.0
NORM_EPS = 1e-6

DIFF_HEADS = 4
DIFF_HEAD_DIM = 64
DIFF_V_DIM = 128
MLA_HEADS = 4
MLA_NOPE = 128
MLA_ROPE = 64
MLA_V = 128
MLA_Q_RANK = 384
MLA_KV_RANK = 256

LANES = 128
LOG2E = math.log2(math.e)
MASK_VALUE = -1e30

ROW_BLOCK = 1024
KEY_ROWS = 512
CHUNK_COLS = 256
SUM_ROWS = 16
VT_ROWS = MLA_V + SUM_ROWS
VMEM_LIMIT_BYTES = 56 * 1024 * 1024

_W_QK = (1024, 0)
_W_LAT = (1024, 2)
_W_DGATE = (512, 3)
_L_CKV, _L_KR, _L_END = MLA_Q_RANK, MLA_Q_RANK + MLA_KV_RANK, MLA_Q_RANK + MLA_KV_RANK + LANES


def _rms(x, gain):
    return x * lax.rsqrt(jnp.mean(x * x, axis=-1, keepdims=True) + NORM_EPS) * gain


def _rope_tile(t, cos, sin_signed, first_half):
    partner = jnp.where(first_half, pltpu.roll(t, 96, axis=1), pltpu.roll(t, 32, axis=1))
    return t * cos + partner * sin_signed


def _rope_tables(pos, invf, lane_group):
    rows = pos.shape[0]
    quarter = rows // 4
    packed = pos[3 * quarter:, :]
    for j in (2, 1, 0):
        packed = jnp.where(lane_group == j, pos[j * quarter:(j + 1) * quarter, :], packed)
    ang = packed * invf

    def spread(tab):
        rolled = [tab] + [pltpu.roll(tab, 32 * k, axis=1) for k in (1, 2, 3)]
        parts = []
        for j in range(4):
            part = rolled[(3 - j) % 4]
            for g in (2, 1, 0):
                part = jnp.where(lane_group == g, rolled[(g - j) % 4], part)
            parts.append(part)
        return jnp.concatenate(parts, axis=0)

    return spread(jnp.cos(ang)), spread(jnp.sin(ang))


def _input_kernel(x_ref, pos_ref, vec_ref, wqk_ref, wvt_ref, wlat_ref, wuq_ref, wuk_ref, wuvt_ref,
                  qd_ref, kd_ref, vtd_ref, qm_ref, km_ref, vtm_ref,
                  *, diff_scale, mla_scale):
    x = x_ref[...]
    gin, gq = vec_ref[0:1, :], vec_ref[1:2, :MLA_Q_RANK]
    gkv, invf = vec_ref[2:3, :MLA_KV_RANK], vec_ref[3:4, :LANES]
    h = _rms(x, gin).astype(jnp.bfloat16)
    lat = jnp.dot(h, wlat_ref[:, :_L_END], preferred_element_type=jnp.float32)

    lane = lax.broadcasted_iota(jnp.int32, (1, LANES), 1)
    first_half = (lane & 63) < 32
    cos, sin = _rope_tables(pos_ref[...].astype(jnp.float32), invf, lane >> 5)
    sin_signed = sin * jnp.where(first_half, -1.0, 1.0)
    rope = functools.partial(_rope_tile, cos=cos, sin_signed=sin_signed, first_half=first_half)
    ones = jnp.ones((SUM_ROWS, x.shape[0]), jnp.bfloat16)
    pad = jnp.zeros((x.shape[0], LANES), jnp.bfloat16)
    for ref in (qd_ref, kd_ref, qm_ref, km_ref):
        ref[:, ref.shape[1] - LANES:] = pad

    nt_dims = (((1,), (1,)), ((), ()))
    proj = jnp.dot(h, wqk_ref[...], preferred_element_type=jnp.float32)
    vt = lax.dot_general(wvt_ref[...], h, nt_dims, preferred_element_type=jnp.float32)
    for hd in range(DIFF_HEADS):
        c = hd * LANES
        q = rope(proj[:, c:c + LANES]) * diff_scale
        qd_ref[:, c:c + LANES] = q.astype(jnp.bfloat16)
        k = rope(proj[:, 512 + c:512 + c + LANES])
        kd_ref[:, c:c + LANES] = k.astype(jnp.bfloat16)
        vtd_ref[hd, :DIFF_V_DIM, :] = vt[c:c + DIFF_V_DIM, :].astype(jnp.bfloat16)
        vtd_ref[hd, DIFF_V_DIM:, :] = ones

    cq = _rms(lat[:, :_L_CKV], gq).astype(jnp.bfloat16)
    qm = jnp.dot(cq, wuq_ref[...], preferred_element_type=jnp.float32)
    ckv = _rms(lat[:, _L_CKV:_L_KR], gkv).astype(jnp.bfloat16)
    kn = jnp.dot(ckv, wuk_ref[...], preferred_element_type=jnp.float32)
    mvt = lax.dot_general(wuvt_ref[...], ckv, nt_dims, preferred_element_type=jnp.float32)
    kr = rope(jnp.where(lane < MLA_ROPE, lat[:, _L_KR:_L_END], 0.0)).astype(jnp.bfloat16)

    for hd in range(MLA_HEADS):
        c = hd * 2 * LANES
        qm_ref[:, c:c + LANES] = (qm[:, c:c + LANES] * mla_scale).astype(jnp.bfloat16)
        qr = rope(qm[:, c + LANES:c + 2 * LANES]) * mla_scale
        qm_ref[:, c + LANES:c + 2 * LANES] = qr.astype(jnp.bfloat16)
        km_ref[:, c:c + LANES] = kn[:, hd * LANES:(hd + 1) * LANES].astype(jnp.bfloat16)
        km_ref[:, c + LANES:c + 2 * LANES] = kr
        vtm_ref[hd, :MLA_V, :] = mvt[hd * MLA_V:(hd + 1) * MLA_V, :].astype(jnp.bfloat16)
        vtm_ref[hd, MLA_V:, :] = ones


def _input_stage(x2, pos2, vecs, w, wvt, wuq, wuk, wuvt):
    s = x2.shape[0]
    tm = ROW_BLOCK
    full = lambda shape: pl.BlockSpec(shape, lambda i: (0,) * len(shape))
    rows = lambda n: pl.BlockSpec((tm, n), lambda i: (i, 0))
    vt_spec = pl.BlockSpec((DIFF_HEADS, VT_ROWS, tm), lambda i: (0, 0, i))
    wcols = lambda blk: pl.BlockSpec((D_MODEL, blk[0]), lambda i: (0, blk[1]))
    kern = functools.partial(
        _input_kernel,
        diff_scale=DIFF_HEAD_DIM ** -0.5 * LOG2E,
        mla_scale=(MLA_NOPE + MLA_ROPE) ** -0.5 * LOG2E)
    bf = jnp.bfloat16
    return pl.pallas_call(
        kern,
        grid=(s // tm,),
        in_specs=[rows(D_MODEL), rows(1), full(vecs.shape), wcols(_W_QK), full(wvt.shape),
                  wcols(_W_LAT), full(wuq.shape), full(wuk.shape), full(wuvt.shape)],
        out_specs=[rows(512 + LANES), rows(512 + LANES), vt_spec, rows(1024 + LANES),
                   rows(1024 + LANES), vt_spec],
        out_shape=[jax.ShapeDtypeStruct((s, 512 + LANES), bf),
                   jax.ShapeDtypeStruct((s, 512 + LANES), bf),
                   jax.ShapeDtypeStruct((DIFF_HEADS, VT_ROWS, s), bf),
                   jax.ShapeDtypeStruct((s, 1024 + LANES), bf),
                   jax.ShapeDtypeStruct((s, 1024 + LANES), bf),
                   jax.ShapeDtypeStruct((MLA_HEADS, VT_ROWS, s), bf)],
        compiler_params=pltpu.CompilerParams(
            dimension_semantics=("parallel",), vmem_limit_bytes=VMEM_LIMIT_BYTES),
        name="input_stage",
    )(x2, pos2, vecs, w, wvt, w, wuq, wuk, wuvt)


_M, _BMAX, _ALPHA = 0, (1, 2), (3, 4)


class _AttnRefs:
    def __init__(self, k_ref, vt_ref, qc_ref, stat_ref, acc_ref, s_refs, p_refs):
        self.k, self.vt, self.qc, self.stat, self.acc = k_ref, vt_ref, qc_ref, stat_ref, acc_ref
        self.s, self.p = s_refs, p_refs

    def row(self, i, cols):
        return self.stat.at[i:i + 1, cols]


def _score_stage(r, start, slot, diag, cols, head, tq):
    dk = r.qc.shape[0]
    tk = r.s[slot].shape[0]
    ncols = cols.stop - cols.start
    j0 = cols.start % tq
    if diag is not None and diag * tk >= j0 + ncols:
        r.s[slot][:, cols] = jnp.full((tk, ncols), MASK_VALUE, jnp.float32)
        r.row(_BMAX[slot], cols)[...] = jnp.full((1, ncols), MASK_VALUE, jnp.float32)
        return
    kb = r.k[pl.ds(start, tk), head * dk:(head + 1) * dk]
    st = jnp.dot(kb, r.qc[:, cols], preferred_element_type=jnp.float32)
    if diag is not None and diag * tk + tk - 1 > j0:
        i = lax.broadcasted_iota(jnp.int32, (tk, ncols), 0) + diag * tk
        j = lax.broadcasted_iota(jnp.int32, (tk, ncols), 1) + j0
        st = jnp.where(i <= j, st, MASK_VALUE)
    r.s[slot][:, cols] = st
    r.row(_BMAX[slot], cols)[...] = jnp.max(st, axis=0, keepdims=True)


def _softmax_stage(r, slot, cols, first=False):
    m_old = (jnp.full((1, cols.stop - cols.start), MASK_VALUE, jnp.float32) if first
             else r.row(_M, cols)[...])
    m_new = jnp.maximum(m_old, r.row(_BMAX[slot], cols)[...])
    r.row(_ALPHA[slot], cols)[...] = jnp.exp2(m_old - m_new)
    r.row(_M, cols)[...] = m_new
    r.p[slot][:, cols] = jnp.exp2(r.s[slot][:, cols] - m_new).astype(jnp.bfloat16)


def _value_stage(r, start, slot, cols, head):
    vtb = r.vt[head, :, pl.ds(start, r.p[slot].shape[0])]
    pv = jnp.dot(vtb, r.p[slot][:, cols], preferred_element_type=jnp.float32)
    r.acc[:, cols] = r.row(_ALPHA[slot], cols)[...] * r.acc[:, cols] + pv


def _attn_kernel(*refs, heads, n_maps, tq, tk, nq):
    n_lam = 1 if n_maps == 2 else 0
    q_ref, k_ref, vt_ref = refs[:3]
    lam_ref = refs[3] if n_lam else None
    o_ref = refs[3 + n_lam]
    qc_ref, stat_ref, acc_ref, s0, s1, p0, p1 = refs[4 + n_lam:]
    r = _AttnRefs(k_ref, vt_ref, qc_ref, stat_ref, acc_ref, (s0, s1), (p0, p1))

    assert tq == 2 * tk
    dk = qc_ref.shape[0]
    ncol = heads * n_maps * tq
    qi = pl.program_id(1)
    prev = jnp.maximum(qi - 1, 0)

    def key_start(b, t):
        blk = jnp.where(t < 2, b * 2 + t, t - 2)
        return pl.multiple_of(blk * tk, tk)

    dv = o_ref.shape[1] // heads

    def normalized(c0):
        cols = slice(c0, c0 + CHUNK_COLS)
        return acc_ref[:dv, cols] * (1.0 / acc_ref[dv:dv + 1, cols])

    def finish_chunk(c0):
        head, g0 = divmod(c0, n_maps * tq)
        if n_maps == 2:
            if g0 < tq:
                return
            lam_init = 0.8 - 0.6 * math.exp(-0.3 * 0)
            lam = (jnp.exp(jnp.sum(lam_ref[0:1, :] * lam_ref[1:2, :], axis=-1, keepdims=True))
                   - jnp.exp(jnp.sum(lam_ref[2:3, :] * lam_ref[3:4, :], axis=-1, keepdims=True))
                   + lam_init)
            oh = normalized(c0 - tq) - lam * normalized(c0)
        else:
            oh = normalized(c0)
        row0 = g0 % tq
        o_ref[row0:row0 + CHUNK_COLS, head * dv:(head + 1) * dv] = oh.T

    def pair_steps(score_starts, value_starts, new_block):
        drain = score_starts is None
        for c0 in range(0, ncol, CHUNK_COLS):
            cols = slice(c0, c0 + CHUNK_COLS)
            head = c0 // (n_maps * tq)
            for step in range(2):
                if not (drain and step == 1):
                    _softmax_stage(r, 1 - step, cols, first=new_block and step == 1)
                if not drain:
                    _score_stage(r, score_starts[step], step, step if new_block else None,
                                 cols, head, tq)
                _value_stage(r, value_starts[step], step, cols, head)
            if new_block or drain:
                finish_chunk(c0)

    prev_a = key_start(prev, 2 * prev)
    prev_b = key_start(prev, 2 * prev + 1)

    @pl.when(qi == 0)
    def _():
        for ref in (s1, p0, p1, stat_ref, acc_ref):
            ref[...] = jnp.zeros(ref.shape, ref.dtype)

    @pl.when(qi < nq)
    def _():
        for h in range(heads):
            qt = q_ref[:, h * dk:(h + 1) * dk].astype(jnp.float32).T
            if n_maps == 2:
                feat = lax.broadcasted_iota(jnp.int32, qt.shape, 0)
                zero = jnp.zeros_like(qt)
                qc_ref[:, 2 * h * tq:(2 * h + 1) * tq] = jnp.where(
                    feat < DIFF_HEAD_DIM, qt, zero).astype(jnp.bfloat16)
                qc_ref[:, (2 * h + 1) * tq:(2 * h + 2) * tq] = jnp.where(
                    feat >= DIFF_HEAD_DIM, qt, zero).astype(jnp.bfloat16)
            else:
                qc_ref[:, h * tq:(h + 1) * tq] = qt.astype(jnp.bfloat16)

        pair_steps((key_start(qi, 0), key_start(qi, 1)), (prev_a, prev_b), new_block=True)

        def pair(u, carry):
            t = 2 * u
            pair_steps((key_start(qi, t), key_start(qi, t + 1)),
                       (key_start(qi, t - 2), key_start(qi, t - 1)), new_block=False)
            return carry

        lax.fori_loop(1, qi + 1, pair, 0)

    @pl.when(qi == nq)
    def _():
        pair_steps(None, (prev_a, prev_b), new_block=False)


def _attention(q, k, vt, lambdas, *, heads, n_maps, dk, name):
    s = q.shape[0]
    n_heads, vt_rows, _ = vt.shape
    dv = vt_rows - SUM_ROWS
    tk, tq = KEY_ROWS, 2 * KEY_ROWS
    nq = s // tq
    ncol = heads * n_maps * tq
    f32, bf = jnp.float32, jnp.bfloat16
    kv_bytes = heads * s * (dk + vt_rows) * 2
    kv_mode = dict(pipeline_mode=pl.Buffered(1)) if 2 * kv_bytes > VMEM_LIMIT_BYTES // 2 else {}
    in_specs = [pl.BlockSpec((tq, heads * dk), lambda h, i: (jnp.minimum(i, nq - 1), h)),
                pl.BlockSpec((s, heads * dk), lambda h, i: (0, h), **kv_mode),
                pl.BlockSpec((heads, vt_rows, s), lambda h, i: (h, 0, 0), **kv_mode)]
    in_specs += [pl.BlockSpec(lam.shape, lambda h, i: (0, 0)) for lam in lambdas]
    wide, wide_acc = ncol + LANES, ncol + 2 * LANES
    scratch = [pltpu.VMEM((dk, wide), bf), pltpu.VMEM((8, ncol), f32),
               pltpu.VMEM((vt_rows, wide_acc), f32),
               pltpu.VMEM((tk, wide), f32), pltpu.VMEM((tk, wide), f32),
               pltpu.VMEM((tk, wide), bf), pltpu.VMEM((tk, wide), bf)]
    return pl.pallas_call(
        functools.partial(_attn_kernel, heads=heads, n_maps=n_maps, tq=tq, tk=tk, nq=nq),
        grid=(n_heads // heads, nq + 1),
        in_specs=in_specs,
        out_specs=pl.BlockSpec((tq, heads * dv), lambda h, i: (jnp.maximum(i - 1, 0), h)),
        out_shape=jax.ShapeDtypeStruct((s, n_heads * dv), jnp.float32),
        scratch_shapes=scratch,
        compiler_params=pltpu.CompilerParams(
            dimension_semantics=("arbitrary", "arbitrary"), vmem_limit_bytes=VMEM_LIMIT_BYTES),
        name=name,
    )(q, k, vt, *lambdas)


def _output_kernel(x_ref, od_ref, om_ref, vec_ref, wdg_ref, wg_ref, wpd_ref, wpm_ref, wout_ref,
                   out_ref, *, lam_init):
    x = x_ref[...]
    gin, gfin, subln = vec_ref[0:1, :], vec_ref[1:2, :], vec_ref[2:3, :LANES]
    h = _rms(x, gin).astype(jnp.bfloat16)
    dgate = jnp.dot(h, wdg_ref[...], preferred_element_type=jnp.float32)
    gates = jnp.dot(h, wg_ref[...], preferred_element_type=jnp.float32)
    mgate, g_diff, g_mla = gates[:, 0:512], gates[:, 512:1536], gates[:, 1536:2560]

    od = od_ref[...]
    od = jnp.concatenate(
        [_rms(od[:, hd * LANES:(hd + 1) * LANES], subln) for hd in range(DIFF_HEADS)], axis=1)
    od = od * (1.0 - lam_init) * jax.nn.silu(dgate)
    om = om_ref[...] * jax.nn.silu(mgate)

    pd = jnp.dot(od.astype(jnp.bfloat16), wpd_ref[...], preferred_element_type=jnp.float32)
    pm = jnp.dot(om.astype(jnp.bfloat16), wpm_ref[...], preferred_element_type=jnp.float32)
    merged = jax.nn.sigmoid(g_diff) * pd + jax.nn.sigmoid(g_mla) * pm
    y = x + jnp.dot(merged.astype(jnp.bfloat16), wout_ref[...],
                    preferred_element_type=jnp.float32)
    out_ref[...] = _rms(y, gfin)


def _output_stage(x2, od, om, vecs, w, wg, wpd, wpm, wout):
    s = x2.shape[0]
    tm = ROW_BLOCK
    full = lambda shape: pl.BlockSpec(shape, lambda i: (0,) * len(shape))
    rows = lambda n: pl.BlockSpec((tm, n), lambda i: (i, 0))
    lam_init = 0.8 - 0.6 * math.exp(-0.3 * 0)
    return pl.pallas_call(
        functools.partial(_output_kernel, lam_init=lam_init),
        grid=(s // tm,),
        in_specs=[rows(D_MODEL), rows(512), rows(512), full(vecs.shape),
                  pl.BlockSpec((D_MODEL, _W_DGATE[0]), lambda i: (0, _W_DGATE[1])),
                  full(wg.shape), full(wpd.shape), full(wpm.shape), full(wout.shape)],
        out_specs=rows(D_MODEL),
        out_shape=jax.ShapeDtypeStruct((s, D_MODEL), jnp.float32),
        compiler_params=pltpu.CompilerParams(
            dimension_semantics=("parallel",), vmem_limit_bytes=VMEM_LIMIT_BYTES),
        name="output_stage",
    )(x2, od, om, vecs, w, wg, wpd, wpm, wout)


def kernel(x, positions, norm_in, w_in, diff_lambda_q1, diff_lambda_k1, diff_lambda_q2,
           diff_lambda_k2, diff_subln, mla_q_norm, w_uq, mla_kv_norm, w_ukv, w_proj_diff,
           w_proj_mla, w_out, norm_final):
    b, s, d = x.shape
    assert b == 1 and d == D_MODEL and w_in.shape[0] == 1
    bf = jnp.bfloat16
    w = w_in[0].astype(bf)
    wg = w[:, 2752:5312]
    wuq = w_uq[0].reshape(MLA_Q_RANK, MLA_HEADS, MLA_NOPE + MLA_ROPE)
    wuq = jnp.concatenate(
        [wuq, jnp.zeros((MLA_Q_RANK, MLA_HEADS, 64), wuq.dtype)], axis=2)
    wuq = wuq.reshape(MLA_Q_RANK, MLA_HEADS * 256).astype(bf)
    wukv = w_ukv[0].reshape(MLA_KV_RANK, MLA_HEADS, MLA_NOPE + MLA_V)
    wuk = wukv[:, :, :MLA_NOPE].reshape(MLA_KV_RANK, MLA_HEADS * MLA_NOPE).astype(bf)
    wuvt = wukv[:, :, MLA_NOPE:].reshape(MLA_KV_RANK, MLA_HEADS * MLA_V).T.astype(bf)
    wvt = w[:, 1024:1536].T

    inv_freq = ROPE_THETA ** (-jnp.arange(0, 64, 2, dtype=jnp.float32) / 64)
    invf = jnp.tile(inv_freq, 4).reshape(1, LANES)

    def vector_rows(*vecs):
        rows = [jnp.pad(v.reshape(1, -1), ((0, 0), (0, d - v.size))) for v in vecs]
        return jnp.pad(jnp.concatenate(rows, axis=0), ((0, 8 - len(rows)), (0, 0)))

    x2 = x.reshape(s, d)
    pos2 = positions.reshape(s, 1)
    qd, kd, vtd, qm, km, vtm = _input_stage(
        x2, pos2, vector_rows(norm_in, mla_q_norm, mla_kv_norm, invf), w, wvt, wuq, wuk, wuvt)

    lam_rows = jnp.concatenate(
        [diff_lambda_q1, diff_lambda_k1, diff_lambda_q2, diff_lambda_k2], axis=0)
    lambdas = (jnp.pad(lam_rows, ((0, 28), (0, LANES - DIFF_HEAD_DIM))),)
    od = _attention(qd, kd, vtd, lambdas, heads=1, n_maps=2, dk=LANES, name="diff_attention")
    om = _attention(qm, km, vtm, (), heads=2, n_maps=1, dk=2 * LANES, name="mla_attention")

    out = _output_stage(x2, od, om, vector_rows(norm_in, norm_final, diff_subln), w, wg,
                        w_proj_diff[0].astype(bf), w_proj_mla[0].astype(bf),
                        w_out[0].astype(bf))
    return out.reshape(b, s, d)
```
